```python
import jax, jax.numpy as jnp
from jax import lax
import numpy as np

D_MODEL = 1024
BATCH = 8
SEQ = 2048
DEPTH = 2

CHUNK = 64
N_MIXERS = 2
N_LAYERS_A = (DEPTH + 1) // 2
N_LAYERS_B = DEPTH // 2
RMS_EPS = 1e-6

GLA_HEADS = 4
KEY_DIM = D_MODEL // 2
VALUE_DIM = D_MODEL
HEAD_K = KEY_DIM // GLA_HEADS
HEAD_V = VALUE_DIM // GLA_HEADS
GATE_RANK = 16
GATE_NORMALIZER = 16.0
PROJ_A = 2 * KEY_DIM + 2 * VALUE_DIM + GATE_RANK
SPLITS_A = (KEY_DIM, 2 * KEY_DIM, 2 * KEY_DIM + VALUE_DIM, 2 * KEY_DIM + 2 * VALUE_DIM)

CONV_DIM = D_MODEL
CONV_WIDTH = 3

D_FF = 2816
FFN_CONV_WIDTH = 3

kernel_name = 'hybrid_gla_shortconv_convffn_trunk'


def _rmsnorm(x, g):
    xf = x.astype(jnp.float32)
    y = xf * lax.rsqrt(jnp.mean(xf * xf, axis=-1, keepdims=True) + RMS_EPS)
    return (y * g.astype(jnp.float32)).astype(x.dtype)


def _causal_dwconv(z, w):
    width, ch = w.shape
    return lax.conv_general_dilated(
        z, w[:, None, :].astype(z.dtype), window_strides=(1,),
        padding=[(width - 1, 0)], dimension_numbers=('NWC', 'WIO', 'NWC'),
        feature_group_count=ch)


def _gla_mixer(x, w_in, w_gate_up, b_gate, gn, w_out):
    bsz, seq, _ = x.shape
    n_chunks = seq // CHUNK
    proj = x @ w_in
    q, k, v, r, gl = jnp.split(proj, SPLITS_A, axis=-1)
    log_a = jax.nn.log_sigmoid((gl @ w_gate_up + b_gate).astype(jnp.float32)) / GATE_NORMALIZER

    def chunked(t, d):
        return t.astype(jnp.float32).reshape(bsz, n_chunks, CHUNK, GLA_HEADS, d)

    q = chunked(q, HEAD_K) * (HEAD_K ** -0.5)
    k = chunked(k, HEAD_K)
    v = chunked(v, HEAD_V)
    log_a = chunked(log_a, HEAD_K)
    cum = jnp.cumsum(log_a, axis=2)
    tot = cum[:, :, -1]
    k_dec = k * jnp.exp(tot[:, :, None] - cum)
    upd = jnp.einsum('bnlhk,bnlhv->nbhkv', k_dec, v)
    q_n = jnp.moveaxis(q, 1, 0)
    decay = jnp.moveaxis(jnp.exp(tot), 1, 0)

    def step(state, inp):
        q_c, d_c, u_c = inp
        state = d_c[..., None] * state + u_c
        return state, jnp.einsum('blhk,bhkv->blhv', q_c, state)

    s0 = jnp.zeros((bsz, GLA_HEADS, HEAD_K, HEAD_V), jnp.float32)
    _, o = lax.scan(step, s0, (q_n, decay, upd))
    o = jnp.moveaxis(o, 0, 1).reshape(bsz, seq, GLA_HEADS, HEAD_V)
    o = o * lax.rsqrt(jnp.mean(o * o, axis=-1, keepdims=True) + RMS_EPS)
    o = o.reshape(bsz, seq, VALUE_DIM) * gn.astype(jnp.float32) * jax.nn.silu(r.astype(jnp.float32))
    return o.astype(x.dtype) @ w_out


def _short_conv_mixer(x, w_in, conv_w, w_out):
    b_gate, c_gate, h = jnp.split(x @ w_in, 3, axis=-1)
    return (b_gate * _causal_dwconv(c_gate * h, conv_w)) @ w_out


def _conv_ffn(x, w_up, conv_w, w_down):
    g, u = jnp.split(x @ w_up, 2, axis=-1)
    return (jax.nn.silu(_causal_dwconv(g, conv_w)) * u) @ w_down


def setup_inputs(seed: int = 0) -> dict:
    key = jax.random.key(seed)
    ks = jax.random.split(key, 20)
    nrm = jax.random.normal
    f32 = jnp.float32
    d = D_MODEL
    return {
        'x': nrm(ks[0], (BATCH, SEQ, d), f32),
        'a_norm': 1.0 + 0.05 * nrm(ks[1], (N_LAYERS_A, d), f32),
        'a_w_in': nrm(ks[2], (N_LAYERS_A, d, PROJ_A), f32) * d ** -0.5,
        'a_w_gate_up': nrm(ks[3], (N_LAYERS_A, GATE_RANK, KEY_DIM), f32) * GATE_RANK ** -0.5,
        'a_b_gate': 0.1 * nrm(ks[4], (N_LAYERS_A, KEY_DIM), f32),
        'a_gn': 1.0 + 0.05 * nrm(ks[5], (N_LAYERS_A, VALUE_DIM), f32),
        'a_w_out': nrm(ks[6], (N_LAYERS_A, VALUE_DIM, d), f32) * VALUE_DIM ** -0.5,
        'b_norm': 1.0 + 0.05 * nrm(ks[7], (N_LAYERS_B, d), f32),
        'b_w_in': nrm(ks[8], (N_LAYERS_B, d, 3 * CONV_DIM), f32) * d ** -0.5,
        'b_conv': nrm(ks[9], (N_LAYERS_B, CONV_WIDTH, CONV_DIM), f32) * CONV_WIDTH ** -0.5,
        'b_w_out': nrm(ks[10], (N_LAYERS_B, CONV_DIM, d), f32) * CONV_DIM ** -0.5,
        'f_norm': 1.0 + 0.05 * nrm(ks[11], (DEPTH, d), f32),
        'f_w_up': nrm(ks[12], (DEPTH, d, 2 * D_FF), f32) * d ** -0.5,
        'f_conv': nrm(ks[13], (DEPTH, FFN_CONV_WIDTH, D_FF), f32) * FFN_CONV_WIDTH ** -0.5,
        'f_w_down': nrm(ks[14], (DEPTH, D_FF, d), f32) * D_FF ** -0.5,
        'final_norm': 1.0 + 0.05 * nrm(ks[15], (d,), f32),
    }


def reference(x, a_norm, a_w_in, a_w_gate_up, a_b_gate, a_gn, a_w_out,
              b_norm, b_w_in, b_conv, b_w_out,
              f_norm, f_w_up, f_conv, f_w_down, final_norm):
    for i in range(DEPTH):
        j = i // N_MIXERS
        if i % N_MIXERS == 0:
            h = _rmsnorm(x, a_norm[j])
            x = x + _gla_mixer(h, a_w_in[j], a_w_gate_up[j], a_b_gate[j], a_gn[j], a_w_out[j])
        else:
            h = _rmsnorm(x, b_norm[j])
            x = x + _short_conv_mixer(h, b_w_in[j], b_conv[j], b_w_out[j])
        h = _rmsnorm(x, f_norm[i])
        x = x + _conv_ffn(h, f_w_up[i], f_conv[i], f_w_down[i])
    return _rmsnorm(x, final_norm)
```

```python
import functools

import jax
import jax.numpy as jnp
from jax import lax
from jax.experimental import pallas as pl
from jax.experimental.pallas import tpu as pltpu

F32 = jnp.float32
BF16 = jnp.bfloat16

CHUNK = 64
GLA_HEADS = 4
GATE_NORMALIZER = 16.0
RMS_EPS = 1e-6
CONV_WIDTH = 3

LANES = 128
SUBLANES = 8
VMEM_BYTES_V7X = 64 * 1024 * 1024

SEQ_TILE = 512
FFN_COL_CHUNK = 256


def _dot(a, b):
    return jnp.dot(a, b, preferred_element_type=F32)


def _rmsnorm(x, g):
    ms = jnp.mean(x * x, axis=-1, keepdims=True)
    return x * lax.rsqrt(ms + RMS_EPS) * g


def _causal_conv3(z, prev, w, row_ids):
    zm1 = jnp.where(row_ids == 0, prev[7:8, :], pltpu.roll(z, 1, 0))
    zm2 = jnp.where(row_ids == 0, prev[6:7, :],
                    jnp.where(row_ids == 1, prev[7:8, :], pltpu.roll(z, 2, 0)))
    return w[0:1, :] * zm2 + w[1:2, :] * zm1 + w[2:3, :] * z


def _log_sigmoid(x):
    return jnp.minimum(x, 0.0) - jnp.log(1.0 + jnp.exp(-jnp.abs(x)))


def _gla_kernel(x_ref, norm_ref, wm_ref, wgl_ref, wgu_ref, bg_ref, gn_ref, wo_ref,
                out_ref, q_s, k_s, v_s, la_s, o_s, state_s, *, key_dim, value_dim):
    ts = x_ref.shape[1]
    head_k = key_dim // GLA_HEADS
    head_v = value_dim // GLA_HEADS

    @pl.when(pl.program_id(1) == 0)
    def _():
        state_s[...] = jnp.zeros_like(state_s)

    x = x_ref[0]
    h = _rmsnorm(x, norm_ref[...]).astype(BF16)
    q_s[...] = (_dot(h, wm_ref[:, 0:key_dim]) * (head_k ** -0.5)).astype(BF16)
    k_s[...] = _dot(h, wm_ref[:, key_dim:2 * key_dim])
    v_s[...] = _dot(h, wm_ref[:, 2 * key_dim:2 * key_dim + value_dim]).astype(BF16)
    gl = _dot(h, wgl_ref[...])
    logits = _dot(gl.astype(BF16), wgu_ref[...]) + bg_ref[...]
    la_s[...] = _log_sigmoid(logits) * (1.0 / GATE_NORMALIZER)

    r_i = lax.broadcasted_iota(jnp.int32, (CHUNK, CHUNK), 0)
    c_i = lax.broadcasted_iota(jnp.int32, (CHUNK, CHUNK), 1)
    tri = jnp.where(r_i >= c_i, 1.0, 0.0).astype(BF16)

    for c in range(ts // CHUNK):
        rows = slice(c * CHUNK, (c + 1) * CHUNK)
        la = la_s[rows, :]
        la_hi = la.astype(BF16)
        la_lo = (la - la_hi.astype(F32)).astype(BF16)
        cum = _dot(tri, la_hi) + _dot(tri, la_lo)
        tot = cum[CHUNK - 1:CHUNK, :]
        k_dec = (k_s[rows, :] * jnp.exp(tot - cum)).astype(BF16)
        decay = jnp.exp(tot)
        for hd in range(GLA_HEADS):
            ks = slice(hd * head_k, (hd + 1) * head_k)
            vs = slice(hd * head_v, (hd + 1) * head_v)
            upd = lax.dot_general(k_dec[:, ks], v_s[rows, vs], (((0,), (0,)), ((), ())),
                                  preferred_element_type=F32)
            d_col = jnp.broadcast_to(decay[:, ks], (head_k, head_k)).T
            d_col = jnp.concatenate([d_col] * (head_v // head_k), axis=1)
            state = state_s[hd] * d_col + upd
            state_s[hd] = state
            o_s[rows, vs] = _dot(q_s[rows, ks], state.astype(BF16))

    r = _dot(h, wm_ref[:, 2 * key_dim + value_dim:])
    gate = gn_ref[...] * (r * jax.nn.sigmoid(r))
    parts = []
    for hd in range(GLA_HEADS):
        vs = slice(hd * head_v, (hd + 1) * head_v)
        o = o_s[:, vs]
        ms = jnp.mean(o * o, axis=-1, keepdims=True)
        parts.append((o * lax.rsqrt(ms + RMS_EPS) * gate[:, vs]).astype(BF16))
    y = jnp.concatenate(parts, axis=1)
    out_ref[0] = x + _dot(y, wo_ref[...])


def _sconv_kernel(x_ref, norm_ref, win_ref, cw_ref, wo_ref, out_ref, carry_s, *, conv_dim):
    ts = x_ref.shape[1]

    @pl.when(pl.program_id(1) == 0)
    def _():
        carry_s[...] = jnp.zeros_like(carry_s)

    x = x_ref[0]
    h = _rmsnorm(x, norm_ref[...]).astype(BF16)
    c_gate = _dot(h, win_ref[:, conv_dim:2 * conv_dim])
    hh = _dot(h, win_ref[:, 2 * conv_dim:])
    z = c_gate * hh
    row_ids = lax.broadcasted_iota(jnp.int32, z.shape, 0)
    conv = _causal_conv3(z, carry_s[...], cw_ref[...], row_ids)
    carry_s[...] = z[ts - SUBLANES:ts, :]
    b_gate = _dot(h, win_ref[:, 0:conv_dim])
    y = (b_gate * conv).astype(BF16)
    out_ref[0] = x + _dot(y, wo_ref[...])


def _ffn_kernel(x_ref, norm_ref, wup_ref, cw_ref, wdn_ref, fnorm_ref, out_ref, act_s, carry_s,
                *, d_ff, final_norm):
    ts = x_ref.shape[1]
    cc = FFN_COL_CHUNK

    @pl.when(pl.program_id(1) == 0)
    def _():
        carry_s[...] = jnp.zeros_like(carry_s)

    x = x_ref[0]
    h = _rmsnorm(x, norm_ref[...]).astype(BF16)
    row_ids = lax.broadcasted_iota(jnp.int32, (ts, cc), 0)
    for ci in range(d_ff // cc):
        cs = slice(ci * cc, (ci + 1) * cc)
        g = _dot(h, wup_ref[:, cs])
        u = _dot(h, wup_ref[:, d_ff + ci * cc:d_ff + (ci + 1) * cc])
        conv = _causal_conv3(g, carry_s[:, cs], cw_ref[:, cs], row_ids)
        carry_s[:, cs] = g[ts - SUBLANES:ts, :]
        act_s[:, cs] = (conv * jax.nn.sigmoid(conv) * u).astype(BF16)
    y = x + _dot(act_s[...], wdn_ref[...])
    if final_norm:
        y = _rmsnorm(y, fnorm_ref[...])
    out_ref[0] = y


def _resident(shape):
    return pl.BlockSpec(shape, lambda b, t: (0,) * len(shape), pipeline_mode=pl.Buffered(1))


def _nbytes(shape, dtype):
    n = 1
    for s in shape:
        n *= s
    return n * jnp.dtype(dtype).itemsize


def _call(kernel, x, consts, scratch, name, temp_bytes):
    bsz, seq, d = x.shape
    ts = SEQ_TILE
    assert seq % ts == 0 and ts % CHUNK == 0
    tile = pl.BlockSpec((1, ts, d), lambda b, t: (b, t, 0))
    resident = sum(_nbytes(c.shape, c.dtype) for c in consts)
    scratch_bytes = sum(_nbytes(s.shape, s.dtype) for s in scratch)
    streamed = 2 * 2 * _nbytes((ts, d), F32)
    vmem = resident + scratch_bytes + streamed + temp_bytes
    assert vmem < VMEM_BYTES_V7X
    return pl.pallas_call(
        kernel,
        out_shape=jax.ShapeDtypeStruct(x.shape, x.dtype),
        grid=(bsz, seq // ts),
        in_specs=[tile] + [_resident(c.shape) for c in consts],
        out_specs=tile,
        scratch_shapes=scratch,
        compiler_params=pltpu.CompilerParams(
            dimension_semantics=("arbitrary", "arbitrary"), vmem_limit_bytes=vmem),
        name=name,
    )(x, *consts)


def _gla_layer(x, norm, w_in, w_gate_up, b_gate, gn, w_out):
    d = x.shape[-1]
    gate_rank, key_dim = w_gate_up.shape
    value_dim = gn.shape[-1]
    main = 2 * key_dim + 2 * value_dim
    ts = SEQ_TILE
    w_main = w_in[:, :main].astype(BF16)
    w_gl = jnp.pad(w_in[:, main:], ((0, 0), (0, LANES - gate_rank))).astype(BF16)
    w_gu = jnp.pad(w_gate_up, ((0, LANES - gate_rank), (0, 0))).astype(BF16)
    consts = [norm.reshape(1, d), w_main, w_gl, w_gu, b_gate.reshape(1, key_dim),
              gn.reshape(1, value_dim), w_out.astype(BF16)]
    scratch = [
        pltpu.VMEM((ts, key_dim), BF16),
        pltpu.VMEM((ts, key_dim), F32),
        pltpu.VMEM((ts, value_dim), BF16),
        pltpu.VMEM((ts, key_dim), F32),
        pltpu.VMEM((ts, value_dim), F32),
        pltpu.VMEM((GLA_HEADS, key_dim // GLA_HEADS, value_dim // GLA_HEADS), F32),
    ]
    temp = 6 * _nbytes((ts, value_dim), F32)
    kern = functools.partial(_gla_kernel, key_dim=key_dim, value_dim=value_dim)
    return _call(kern, x, consts, scratch, "gla_layer", temp)


def _sconv_layer(x, norm, w_in, conv_w, w_out):
    d = x.shape[-1]
    conv_dim = conv_w.shape[-1]
    consts = [norm.reshape(1, d), w_in.astype(BF16), conv_w, w_out.astype(BF16)]
    scratch = [pltpu.VMEM((SUBLANES, conv_dim), F32)]
    temp = 8 * _nbytes((SEQ_TILE, conv_dim), F32)
    kern = functools.partial(_sconv_kernel, conv_dim=conv_dim)
    return _call(kern, x, consts, scratch, "sconv_layer", temp)


def _ffn_layer(x, norm, w_up, conv_w, w_down, final_norm_w, final_norm):
    d = x.shape[-1]
    d_ff = conv_w.shape[-1]
    assert d_ff % FFN_COL_CHUNK == 0
    consts = [norm.reshape(1, d), w_up.astype(BF16), conv_w, w_down.astype(BF16),
              final_norm_w.reshape(1, d)]
    scratch = [pltpu.VMEM((SEQ_TILE, d_ff), BF16), pltpu.VMEM((SUBLANES, d_ff), F32)]
    temp = 6 * _nbytes((SEQ_TILE, d), F32)
    kern = functools.partial(_ffn_kernel, d_ff=d_ff, final_norm=final_norm)
    return _call(kern, x, consts, scratch, "ffn_final" if final_norm else "ffn_layer", temp)


def kernel(x, a_norm, a_w_in, a_w_gate_up, a_b_gate, a_gn, a_w_out, b_norm, b_w_in, b_conv, b_w_out,
           f_norm, f_w_up, f_conv, f_w_down, final_norm):
    depth = f_norm.shape[0]
    n_mixers = 2
    for i in range(depth):
        j = i // n_mixers
        if i % n_mixers == 0:
            x = _gla_layer(x, a_norm[j], a_w_in[j], a_w_gate_up[j], a_b_gate[j], a_gn[j], a_w_out[j])
        else:
            x = _sconv_layer(x, b_norm[j], b_w_in[j], b_conv[j], b_w_out[j])
        x = _ffn_layer(x, f_norm[i], f_w_up[i], f_conv[i], f_w_down[i], final_norm,
                       final_norm=(i == depth - 1))
    return x
```

```python
import functools

import jax
import jax.numpy as jnp
from jax import lax
from jax.experimental import pallas as pl
from jax.experimental.pallas import tpu as pltpu

F32 = jnp.float32
BF16 = jnp.bfloat16

CHUNK = 64
GLA_HEADS = 4
GATE_NORMALIZER = 16.0
RMS_EPS = 1e-6
N_MIXERS = 2

SUBLANES = 8
LANES = 128
VMEM_BYTES_V7X = 64 * 1024 * 1024

SEQ_TILE = 512
FFN_COL_CHUNK = 256


def _dot(a, b):
    return lax.dot_general(a, b, (((1,), (0,)), ((), ())), preferred_element_type=F32)


def _rmsnorm(x, g):
    ms = jnp.mean(x * x, axis=-1, keepdims=True)
    return x * lax.rsqrt(ms + RMS_EPS) * g


def _causal_conv3(z, prev, w, row_ids):
    zm1 = jnp.where(row_ids == 0, prev[7:8, :], pltpu.roll(z, 1, 0))
    zm2 = jnp.where(row_ids == 0, prev[6:7, :],
                    jnp.where(row_ids == 1, prev[7:8, :], pltpu.roll(z, 2, 0)))
    return w[0:1, :] * zm2 + w[1:2, :] * zm1 + w[2:3, :] * z


def _log_sigmoid(x):
    return jnp.minimum(x, 0.0) - jnp.log(1.0 + jnp.exp(-jnp.abs(x)))


def _gla_kernel(x_ref, norm_ref, win_ref, wgu_ref, bg_ref, gn_ref, wo_ref, out_ref,
                h_s, q_s, k_s, v_s, la_s, cum_s, kd_s, kdt_s, gate_s, o_s, upd_s, dcol_s, state_s,
                *, key_dim, value_dim):
    ts = x_ref.shape[1]
    n_chunks = ts // CHUNK
    head_k = key_dim // GLA_HEADS
    head_v = value_dim // GLA_HEADS
    main = 2 * key_dim + 2 * value_dim

    @pl.when(pl.program_id(1) == 0)
    def _():
        state_s[...] = jnp.zeros_like(state_s)

    x = x_ref[0]
    h_s[...] = _rmsnorm(x, norm_ref[...]).astype(BF16)
    gl = _dot(h_s[...], win_ref[:, main:])
    q_s[...] = (_dot(h_s[...], win_ref[:, 0:key_dim]) * (head_k ** -0.5)).astype(BF16)
    logits = _dot(gl.astype(BF16), wgu_ref[...]) + bg_ref[...]
    k_s[...] = _dot(h_s[...], win_ref[:, key_dim:2 * key_dim])
    la_s[...] = _log_sigmoid(logits) * (1.0 / GATE_NORMALIZER)
    v_s[...] = _dot(h_s[...], win_ref[:, 2 * key_dim:2 * key_dim + value_dim]).astype(BF16)

    r_i = lax.broadcasted_iota(jnp.int32, (CHUNK, CHUNK), 0)
    c_i = lax.broadcasted_iota(jnp.int32, (CHUNK, CHUNK), 1)
    tri = jnp.where(r_i >= c_i, 1.0, 0.0).astype(BF16)

    for c in range(n_chunks):
        rows = slice(c * CHUNK, (c + 1) * CHUNK)
        la = la_s[rows, :]
        la_hi = la.astype(BF16)
        la_lo = (la - la_hi.astype(F32)).astype(BF16)
        cum_s[rows, :] = _dot(tri, la_hi) + _dot(tri, la_lo)
    r = _dot(h_s[...], win_ref[:, 2 * key_dim + value_dim:main])
    gate_s[...] = gn_ref[...] * (r * jax.nn.sigmoid(r))
    for c in range(n_chunks):
        rows = slice(c * CHUNK, (c + 1) * CHUNK)
        cum = cum_s[rows, :]
        tot = cum[CHUNK - 1:CHUNK, :]
        kd_s[rows, :] = k_s[rows, :] * jnp.exp(tot - cum)
        decay = jnp.exp(tot)
        for hd in range(GLA_HEADS):
            ks = slice(hd * head_k, (hd + 1) * head_k)
            dcol_s[c, hd] = jnp.broadcast_to(decay[:, ks], (head_k, head_k)).T
    kdt_s[...] = kd_s[...].T
    lane = lax.broadcasted_iota(jnp.int32, (head_k, LANES), 1)
    for c in range(n_chunks):
        blk = slice((c * CHUNK) // LANES * LANES, ((c * CHUNK) // LANES + 1) * LANES)
        lo = (c * CHUNK) % LANES
        if lo == 0:
            in_chunk = lane < CHUNK
        elif lo + CHUNK == LANES:
            in_chunk = lane >= lo
        else:
            in_chunk = (lane >= lo) & (lane < lo + CHUNK)
        for hd in range(GLA_HEADS):
            ks = slice(hd * head_k, (hd + 1) * head_k)
            vs = slice(hd * head_v, (hd + 1) * head_v)
            k_dec_t = jnp.where(in_chunk, kdt_s[ks, blk], 0.0).astype(BF16)
            upd_s[c, hd] = _dot(k_dec_t, v_s[blk, vs])

    for hd in range(GLA_HEADS):
        ks = slice(hd * head_k, (hd + 1) * head_k)
        vs = slice(hd * head_v, (hd + 1) * head_v)
        state = state_s[hd]
        for c in range(n_chunks):
            rows = slice(c * CHUNK, (c + 1) * CHUNK)
            d_col = jnp.concatenate([dcol_s[c, hd]] * (head_v // head_k), axis=1)
            state = state * d_col + upd_s[c, hd]
            o_s[rows, vs] = _dot(q_s[rows, ks], state.astype(BF16))
        state_s[hd] = state

    parts = []
    for hd in range(GLA_HEADS):
        vs = slice(hd * head_v, (hd + 1) * head_v)
        o = o_s[:, vs]
        ms = jnp.mean(o * o, axis=-1, keepdims=True)
        parts.append((o * lax.rsqrt(ms + RMS_EPS) * gate_s[:, vs]).astype(BF16))
    y = jnp.concatenate(parts, axis=1)
    out_ref[0] = x + _dot(y, wo_ref[...])


def _sconv_kernel(x_ref, norm_ref, win_ref, cw_ref, wo_ref, out_ref, carry_s, *, conv_dim):
    ts = x_ref.shape[1]

    @pl.when(pl.program_id(1) == 0)
    def _():
        carry_s[...] = jnp.zeros_like(carry_s)

    x = x_ref[0]
    h = _rmsnorm(x, norm_ref[...]).astype(BF16)
    c_gate = _dot(h, win_ref[:, conv_dim:2 * conv_dim])
    hh = _dot(h, win_ref[:, 2 * conv_dim:])
    z = c_gate * hh
    row_ids = lax.broadcasted_iota(jnp.int32, z.shape, 0)
    conv = _causal_conv3(z, carry_s[...], cw_ref[...], row_ids)
    carry_s[...] = z[ts - SUBLANES:ts, :]
    b_gate = _dot(h, win_ref[:, 0:conv_dim])
    y = (b_gate * conv).astype(BF16)
    out_ref[0] = x + _dot(y, wo_ref[...])


def _ffn_kernel(x_ref, norm_ref, wup_ref, cw_ref, wdn_ref, fnorm_ref, out_ref, act_s, carry_s,
                *, d_ff, final_norm):
    ts = x_ref.shape[1]
    cc = FFN_COL_CHUNK

    @pl.when(pl.program_id(1) == 0)
    def _():
        carry_s[...] = jnp.zeros_like(carry_s)

    x = x_ref[0]
    h = _rmsnorm(x, norm_ref[...]).astype(BF16)
    row_ids = lax.broadcasted_iota(jnp.int32, (ts, cc), 0)
    for ci in range(d_ff // cc):
        cs = slice(ci * cc, (ci + 1) * cc)
        g = _dot(h, wup_ref[:, cs])
        u = _dot(h, wup_ref[:, d_ff + ci * cc:d_ff + (ci + 1) * cc])
        conv = _causal_conv3(g, carry_s[:, cs], cw_ref[:, cs], row_ids)
        carry_s[:, cs] = g[ts - SUBLANES:ts, :]
        act_s[:, cs] = (conv * jax.nn.sigmoid(conv) * u).astype(BF16)
    y = x + _dot(act_s[...], wdn_ref[...])
    if final_norm:
        y = _rmsnorm(y, fnorm_ref[...])
    out_ref[0] = y


def _nbytes(shape, dtype):
    n = 1
    for s in shape:
        n *= s
    return n * jnp.dtype(dtype).itemsize


def _layer_param(stacked, layer):
    if stacked.ndim == 2:
        stacked = stacked.reshape(stacked.shape[0], 1, stacked.shape[1])
    spec = pl.BlockSpec((None,) + stacked.shape[1:], lambda b, t: (layer, 0, 0),
                        pipeline_mode=pl.Buffered(1))
    return stacked, spec


def _call(kernel, x, params, scratch, name, temp_bytes):
    bsz, seq, d = x.shape
    ts = SEQ_TILE
    assert seq % ts == 0 and ts % LANES == 0 and LANES % CHUNK == 0
    tile = pl.BlockSpec((1, ts, d), lambda b, t: (b, t, 0))
    arrays = [p[0] for p in params]
    resident = sum(_nbytes(a.shape[1:], a.dtype) for a in arrays)
    scratch_bytes = sum(_nbytes(s.shape, s.dtype) for s in scratch)
    streamed = 2 * 2 * _nbytes((ts, d), F32)
    vmem = resident + scratch_bytes + streamed + temp_bytes
    assert vmem < VMEM_BYTES_V7X, vmem
    return pl.pallas_call(
        kernel,
        out_shape=jax.ShapeDtypeStruct(x.shape, x.dtype),
        grid=(bsz, seq // ts),
        in_specs=[tile] + [p[1] for p in params],
        out_specs=tile,
        scratch_shapes=scratch,
        compiler_params=pltpu.CompilerParams(
            dimension_semantics=("arbitrary", "arbitrary"), vmem_limit_bytes=vmem),
        name=name,
    )(x, *arrays)


def _gla_layer(x, j, norm, w_in, w_gate_up, b_gate, gn, w_out):
    key_dim = w_gate_up.shape[-1]
    value_dim = gn.shape[-1]
    head_k, head_v = key_dim // GLA_HEADS, value_dim // GLA_HEADS
    ts = SEQ_TILE
    params = [_layer_param(p, j) for p in (norm, w_in, w_gate_up, b_gate, gn, w_out)]
    scratch = [
        pltpu.VMEM((ts, x.shape[-1]), BF16),
        pltpu.VMEM((ts, key_dim), BF16),
        pltpu.VMEM((ts, key_dim), F32),
        pltpu.VMEM((ts, value_dim), BF16),
        pltpu.VMEM((ts, key_dim), F32),
        pltpu.VMEM((ts, key_dim), F32),
        pltpu.VMEM((ts, key_dim), F32),
        pltpu.VMEM((key_dim, ts), F32),
        pltpu.VMEM((ts, value_dim), F32),
        pltpu.VMEM((ts, value_dim), F32),
        pltpu.VMEM((ts // CHUNK, GLA_HEADS, head_k, head_v), F32),
        pltpu.VMEM((ts // CHUNK, GLA_HEADS, head_k, head_k), F32),
        pltpu.VMEM((GLA_HEADS, head_k, head_v), F32),
    ]
    temp = 6 * _nbytes((ts, value_dim), F32)
    kern = functools.partial(_gla_kernel, key_dim=key_dim, value_dim=value_dim)
    return _call(kern, x, params, scratch, "gla_layer", temp)


def _sconv_layer(x, j, norm, w_in, conv_w, w_out):
    conv_dim = conv_w.shape[-1]
    params = [_layer_param(p, j) for p in (norm, w_in, conv_w, w_out)]
    scratch = [pltpu.VMEM((SUBLANES, conv_dim), F32)]
    temp = 8 * _nbytes((SEQ_TILE, conv_dim), F32)
    kern = functools.partial(_sconv_kernel, conv_dim=conv_dim)
    return _call(kern, x, params, scratch, "sconv_layer", temp)


def _ffn_layer(x, i, norm, w_up, conv_w, w_down, final_norm_w, final_norm):
    d = x.shape[-1]
    d_ff = conv_w.shape[-1]
    assert d_ff % FFN_COL_CHUNK == 0
    params = [_layer_param(p, i) for p in (norm, w_up, conv_w, w_down)]
    params.append(_layer_param(final_norm_w.reshape(1, d), 0))
    scratch = [pltpu.VMEM((SEQ_TILE, d_ff), BF16), pltpu.VMEM((SUBLANES, d_ff), F32)]
    temp = 6 * _nbytes((SEQ_TILE, d), F32)
    kern = functools.partial(_ffn_kernel, d_ff=d_ff, final_norm=final_norm)
    return _call(kern, x, params, scratch, "ffn_final" if final_norm else "ffn_layer", temp)


def kernel(x, a_norm, a_w_in, a_w_gate_up, a_b_gate, a_gn, a_w_out, b_norm, b_w_in, b_conv, b_w_out,
           f_norm, f_w_up, f_conv, f_w_down, final_norm):
    depth = f_norm.shape[0]
    for i in range(depth):
        j = i // N_MIXERS
        if i % N_MIXERS == 0:
            x = _gla_layer(x, j, a_norm, a_w_in, a_w_gate_up, a_b_gate, a_gn, a_w_out)
        else:
            x = _sconv_layer(x, j, b_norm, b_w_in, b_conv, b_w_out)
        x = _ffn_layer(x, i, f_norm, f_w_up, f_conv, f_w_down, final_norm,
                       final_norm=(i == depth - 1))
    return x
```

```python
import functools

import jax
import jax.numpy as jnp
from jax import lax
from jax.experimental import pallas as pl
from jax.experimental.pallas import tpu as pltpu

F32 = jnp.float32
BF16 = jnp.bfloat16

CHUNK = 64
GLA_HEADS = 4
GATE_NORMALIZER = 16.0
RMS_EPS = 1e-6
N_MIXERS = 2

SUBLANES = 8
LANES = 128
VMEM_BYTES_V7X = 64 * 1024 * 1024

SEQ_TILE = 512
SCONV_SEQ_TILE = 1024


def _dot(a, b):
    return lax.dot_general(a, b, (((1,), (0,)), ((), ())), preferred_element_type=F32)


def _dot_nt(a, bt):
    return lax.dot_general(a, bt, (((1,), (1,)), ((), ())), preferred_element_type=F32)


def _rmsnorm(x, g):
    ms = jnp.mean(x * x, axis=-1, keepdims=True)
    return x * lax.rsqrt(ms + RMS_EPS) * g


def _causal_conv3(z, prev, w, row_ids):
    zm1 = jnp.where(row_ids == 0, prev[7:8, :], pltpu.roll(z, 1, 0))
    zm2 = jnp.where(row_ids == 0, prev[6:7, :],
                    jnp.where(row_ids == 1, prev[7:8, :], pltpu.roll(z, 2, 0)))
    return w[0:1, :] * zm2 + w[1:2, :] * zm1 + w[2:3, :] * z


def _log_sigmoid(x):
    return jnp.minimum(x, 0.0) - jnp.log(1.0 + jnp.exp(-jnp.abs(x)))


def _gla_kernel(x_ref, norm_ref, wint_ref, wgu_ref, bg_ref, gn_ref, wo_ref, out_ref,
                h_s, q_s, k_s, v_s, la_s, cum_s, kd_s, kdt_s, gate_s, o_s, upd_s, dcol_s, state_s,
                *, key_dim, value_dim):
    ts = x_ref.shape[1]
    n_chunks = ts // CHUNK
    head_k = key_dim // GLA_HEADS
    head_v = value_dim // GLA_HEADS
    main = 2 * key_dim + 2 * value_dim

    @pl.when(pl.program_id(1) == 0)
    def _():
        state_s[...] = jnp.zeros_like(state_s)

    x = x_ref[0]
    h_s[...] = _rmsnorm(x, norm_ref[...]).astype(BF16)
    gl = _dot_nt(h_s[...], wint_ref[main:, :])
    qk = _dot_nt(h_s[...], wint_ref[0:2 * key_dim, :])
    q_s[...] = (qk[:, :key_dim] * (head_k ** -0.5)).astype(BF16)
    k_s[...] = qk[:, key_dim:]
    logits = _dot(gl.astype(BF16), wgu_ref[...]) + bg_ref[...]
    la_s[...] = _log_sigmoid(logits) * (1.0 / GATE_NORMALIZER)
    v_s[...] = _dot_nt(h_s[...], wint_ref[2 * key_dim:2 * key_dim + value_dim, :]).astype(BF16)

    r_i = lax.broadcasted_iota(jnp.int32, (CHUNK, CHUNK), 0)
    c_i = lax.broadcasted_iota(jnp.int32, (CHUNK, CHUNK), 1)
    tri = jnp.where(r_i >= c_i, 1.0, 0.0).astype(BF16)

    for c in range(n_chunks):
        rows = slice(c * CHUNK, (c + 1) * CHUNK)
        la = la_s[rows, :]
        la_hi = la.astype(BF16)
        la_lo = (la - la_hi.astype(F32)).astype(BF16)
        cum_s[rows, :] = _dot(tri, la_hi) + _dot(tri, la_lo)
    r = _dot_nt(h_s[...], wint_ref[2 * key_dim + value_dim:main, :])
    gate_s[...] = gn_ref[...] * (r * jax.nn.sigmoid(r))
    for c in range(n_chunks):
        rows = slice(c * CHUNK, (c + 1) * CHUNK)
        cum = cum_s[rows, :]
        tot = cum[CHUNK - 1:CHUNK, :]
        kd_s[rows, :] = k_s[rows, :] * jnp.exp(tot - cum)
        decay = jnp.exp(tot)
        for hd in range(GLA_HEADS):
            ks = slice(hd * head_k, (hd + 1) * head_k)
            dcol_s[c, hd] = jnp.broadcast_to(decay[:, ks], (head_k, head_k)).T
    kdt_s[...] = kd_s[...].T
    lane = lax.broadcasted_iota(jnp.int32, (head_k, LANES), 1)
    for c in range(n_chunks):
        blk = slice((c * CHUNK) // LANES * LANES, ((c * CHUNK) // LANES + 1) * LANES)
        lo = (c * CHUNK) % LANES
        if lo == 0:
            in_chunk = lane < CHUNK
        elif lo + CHUNK == LANES:
            in_chunk = lane >= lo
        else:
            in_chunk = (lane >= lo) & (lane < lo + CHUNK)
        for hd in range(GLA_HEADS):
            ks = slice(hd * head_k, (hd + 1) * head_k)
            vs = slice(hd * head_v, (hd + 1) * head_v)
            k_dec_t = jnp.where(in_chunk, kdt_s[ks, blk], 0.0).astype(BF16)
            upd_s[c, hd] = _dot(k_dec_t, v_s[blk, vs])

    for hd in range(GLA_HEADS):
        ks = slice(hd * head_k, (hd + 1) * head_k)
        vs = slice(hd * head_v, (hd + 1) * head_v)
        state = state_s[hd]
        for c in range(n_chunks):
            rows = slice(c * CHUNK, (c + 1) * CHUNK)
            d_col = jnp.concatenate([dcol_s[c, hd]] * (head_v // head_k), axis=1)
            state = state * d_col + upd_s[c, hd]
            o_s[rows, vs] = _dot(q_s[rows, ks], state.astype(BF16))
        state_s[hd] = state

    parts = []
    for hd in range(GLA_HEADS):
        vs = slice(hd * head_v, (hd + 1) * head_v)
        o = o_s[:, vs]
        ms = jnp.mean(o * o, axis=-1, keepdims=True)
        parts.append((o * lax.rsqrt(ms + RMS_EPS) * gate_s[:, vs]).astype(BF16))
    y = jnp.concatenate(parts, axis=1)
    out_ref[0] = x + _dot(y, wo_ref[...])


def _sconv_kernel(x_ref, norm_ref, win_ref, cw_ref, wo_ref, out_ref, carry_s, *, conv_dim):
    ts = x_ref.shape[1]

    @pl.when(pl.program_id(1) == 0)
    def _():
        carry_s[...] = jnp.zeros_like(carry_s)

    x = x_ref[0]
    h = _rmsnorm(x, norm_ref[...]).astype(BF16)
    ch = _dot(h, win_ref[:, conv_dim:])
    z = ch[:, :conv_dim] * ch[:, conv_dim:]
    row_ids = lax.broadcasted_iota(jnp.int32, z.shape, 0)
    conv = _causal_conv3(z, carry_s[...], cw_ref[...], row_ids)
    carry_s[...] = z[ts - SUBLANES:ts, :]
    b_gate = _dot(h, win_ref[:, 0:conv_dim])
    y = (b_gate * conv).astype(BF16)
    out_ref[0] = x + _dot(y, wo_ref[...])


def _ffn_kernel(x_ref, norm_ref, wup_ref, cw_ref, wdn_ref, fnorm_ref, out_ref, act_s, carry_s,
                *, d_ff, final_norm):
    ts = x_ref.shape[1]

    @pl.when(pl.program_id(1) == 0)
    def _():
        carry_s[...] = jnp.zeros_like(carry_s)

    x = x_ref[0]
    h = _rmsnorm(x, norm_ref[...]).astype(BF16)
    gu = _dot(h, wup_ref[...])
    g = gu[:, :d_ff]
    row_ids = lax.broadcasted_iota(jnp.int32, g.shape, 0)
    conv = _causal_conv3(g, carry_s[...], cw_ref[...], row_ids)
    carry_s[...] = g[ts - SUBLANES:ts, :]
    act_s[...] = (conv * jax.nn.sigmoid(conv) * gu[:, d_ff:]).astype(BF16)
    y = x + _dot(act_s[...], wdn_ref[...])
    if final_norm:
        y = _rmsnorm(y, fnorm_ref[...])
    out_ref[0] = y


def _nbytes(shape, dtype):
    n = 1
    for s in shape:
        n *= s
    return n * jnp.dtype(dtype).itemsize


def _layer_param(stacked, layer):
    if stacked.ndim == 2:
        stacked = stacked.reshape(stacked.shape[0], 1, stacked.shape[1])
    spec = pl.BlockSpec((None,) + stacked.shape[1:], lambda b, t: (layer, 0, 0),
                        pipeline_mode=pl.Buffered(1))
    return stacked, spec, _nbytes(stacked.shape[1:], stacked.dtype)


def _call(kernel, x, params, scratch, name, temp_bytes, ts=SEQ_TILE):
    bsz, seq, d = x.shape
    assert seq % ts == 0 and ts % LANES == 0 and LANES % CHUNK == 0
    tile = pl.BlockSpec((1, ts, d), lambda b, t: (b, t, 0))
    arrays = [p[0] for p in params]
    resident = sum(p[2] for p in params)
    scratch_bytes = sum(_nbytes(s.shape, s.dtype) for s in scratch)
    streamed = 2 * 2 * _nbytes((ts, d), F32)
    vmem = resident + scratch_bytes + streamed + temp_bytes
    assert vmem < VMEM_BYTES_V7X, vmem
    return pl.pallas_call(
        kernel,
        out_shape=jax.ShapeDtypeStruct(x.shape, x.dtype),
        grid=(bsz, seq // ts),
        in_specs=[tile] + [p[1] for p in params],
        out_specs=tile,
        scratch_shapes=scratch,
        compiler_params=pltpu.CompilerParams(
            dimension_semantics=("arbitrary", "arbitrary"), vmem_limit_bytes=vmem),
        name=name,
    )(x, *arrays)


def _gla_layer(x, j, norm, w_in, w_gate_up, b_gate, gn, w_out):
    key_dim = w_gate_up.shape[-1]
    value_dim = gn.shape[-1]
    head_k, head_v = key_dim // GLA_HEADS, value_dim // GLA_HEADS
    ts = SEQ_TILE
    w_in_t = jnp.swapaxes(w_in, 1, 2)
    params = [_layer_param(p, j) for p in (norm, w_in_t, w_gate_up, b_gate, gn, w_out)]
    scratch = [
        pltpu.VMEM((ts, x.shape[-1]), BF16),
        pltpu.VMEM((ts, key_dim), BF16),
        pltpu.VMEM((ts, key_dim), F32),
        pltpu.VMEM((ts, value_dim), BF16),
        pltpu.VMEM((ts, key_dim), F32),
        pltpu.VMEM((ts, key_dim), F32),
        pltpu.VMEM((ts, key_dim), F32),
        pltpu.VMEM((key_dim, ts), F32),
        pltpu.VMEM((ts, value_dim), F32),
        pltpu.VMEM((ts, value_dim), F32),
        pltpu.VMEM((ts // CHUNK, GLA_HEADS, head_k, head_v), F32),
        pltpu.VMEM((ts // CHUNK, GLA_HEADS, head_k, head_k), F32),
        pltpu.VMEM((GLA_HEADS, head_k, head_v), F32),
    ]
    temp = 6 * _nbytes((ts, value_dim), F32)
    kern = functools.partial(_gla_kernel, key_dim=key_dim, value_dim=value_dim)
    return _call(kern, x, params, scratch, "gla_layer", temp)


def _sconv_layer(x, j, norm, w_in, conv_w, w_out):
    conv_dim = conv_w.shape[-1]
    params = [_layer_param(p, j) for p in (norm, w_in, conv_w, w_out)]
    scratch = [pltpu.VMEM((SUBLANES, conv_dim), F32)]
    temp = 6 * _nbytes((SCONV_SEQ_TILE, conv_dim), F32)
    kern = functools.partial(_sconv_kernel, conv_dim=conv_dim)
    return _call(kern, x, params, scratch, "sconv_layer", temp, ts=SCONV_SEQ_TILE)


def _ffn_layer(x, i, norm, w_up, conv_w, w_down, final_norm_w, final_norm):
    d = x.shape[-1]
    d_ff = conv_w.shape[-1]
    params = [_layer_param(p, i) for p in (norm, w_up, conv_w, w_down)]
    params.append(_layer_param(final_norm_w.reshape(1, d), 0))
    scratch = [
        pltpu.VMEM((SEQ_TILE, d_ff), BF16),
        pltpu.VMEM((SUBLANES, d_ff), F32),
    ]
    temp = 6 * _nbytes((SEQ_TILE, d), F32)
    kern = functools.partial(_ffn_kernel, d_ff=d_ff, final_norm=final_norm)
    return _call(kern, x, params, scratch, "ffn_final" if final_norm else "ffn_layer", temp)


def kernel(x, a_norm, a_w_in, a_w_gate_up, a_b_gate, a_gn, a_w_out, b_norm, b_w_in, b_conv, b_w_out,
           f_norm, f_w_up, f_conv, f_w_down, final_norm):
    depth = f_norm.shape[0]
    for i in range(depth):
        j = i // N_MIXERS
        if i % N_MIXERS == 0:
            x = _gla_layer(x, j, a_norm, a_w_in, a_w_gate_up, a_b_gate, a_gn, a_w_out)
        else:
            x = _sconv_layer(x, j, b_norm, b_w_in, b_conv, b_w_out)
        x = _ffn_layer(x, i, f_norm, f_w_up, f_conv, f_w_down, final_norm,
                       final_norm=(i == depth - 1))
    return x
```

```python
import functools

import jax
import jax.numpy as jnp
from jax import lax
from jax.experimental import pallas as pl
from jax.experimental.pallas import tpu as pltpu

F32 = jnp.float32
BF16 = jnp.bfloat16

CHUNK = 64
GLA_HEADS = 4
GATE_NORMALIZER = 16.0
RMS_EPS = 1e-6
N_MIXERS = 2

SUBLANES = 8
LANES = 128
VMEM_BYTES_V7X = 64 * 1024 * 1024

SEQ_TILE = 512
SCONV_SEQ_TILE = 1024


def _dot(a, b):
    return lax.dot_general(a, b, (((1,), (0,)), ((), ())), preferred_element_type=F32)


def _dot_nt(a, bt):
    return lax.dot_general(a, bt, (((1,), (1,)), ((), ())), preferred_element_type=F32)


def _rmsnorm(x, g):
    ms = jnp.mean(x * x, axis=-1, keepdims=True)
    return x * lax.rsqrt(ms + RMS_EPS) * g


def _causal_conv3(z, prev, w, row_ids):
    zm1 = jnp.where(row_ids == 0, prev[7:8, :], pltpu.roll(z, 1, 0))
    zm2 = jnp.where(row_ids == 0, prev[6:7, :],
                    jnp.where(row_ids == 1, prev[7:8, :], pltpu.roll(z, 2, 0)))
    return w[0:1, :] * zm2 + w[1:2, :] * zm1 + w[2:3, :] * z


def _log_sigmoid(x):
    return jnp.minimum(x, 0.0) - jnp.log(1.0 + jnp.exp(-jnp.abs(x)))


def _gla_kernel(x_ref, norm_ref, wint_ref, wgu_ref, bg_ref, gn_ref, wo_ref, out_ref,
                h_s, q_s, k_s, v_s, la_s, cum_s, kd_s, kdt_s, gate_s, o_s, upd_s, dcol_s, state_s,
                *, layer, key_dim, value_dim):
    ts = x_ref.shape[1]
    vec = slice(layer, layer + 1)
    n_chunks = ts // CHUNK
    head_k = key_dim // GLA_HEADS
    head_v = value_dim // GLA_HEADS
    main = 2 * key_dim + 2 * value_dim

    @pl.when(pl.program_id(1) == 0)
    def _():
        state_s[...] = jnp.zeros_like(state_s)

    x = x_ref[0]
    h_s[...] = _rmsnorm(x, norm_ref[vec, :]).astype(BF16)
    gl = _dot_nt(h_s[...], wint_ref[main:, :])
    qk = _dot_nt(h_s[...], wint_ref[0:2 * key_dim, :])
    q_s[...] = (qk[:, :key_dim] * (head_k ** -0.5)).astype(BF16)
    k_s[...] = qk[:, key_dim:]
    logits = _dot(gl.astype(BF16), wgu_ref[...]) + bg_ref[vec, :]
    la_s[...] = _log_sigmoid(logits) * (1.0 / GATE_NORMALIZER)
    v_s[...] = _dot_nt(h_s[...], wint_ref[2 * key_dim:2 * key_dim + value_dim, :]).astype(BF16)

    r_i = lax.broadcasted_iota(jnp.int32, (CHUNK, CHUNK), 0)
    c_i = lax.broadcasted_iota(jnp.int32, (CHUNK, CHUNK), 1)
    tri = jnp.where(r_i >= c_i, 1.0, 0.0).astype(BF16)

    for c in range(n_chunks):
        rows = slice(c * CHUNK, (c + 1) * CHUNK)
        la = la_s[rows, :]
        la_hi = la.astype(BF16)
        la_lo = (la - la_hi.astype(F32)).astype(BF16)
        cum_s[rows, :] = _dot(tri, la_hi) + _dot(tri, la_lo)
    r = _dot_nt(h_s[...], wint_ref[2 * key_dim + value_dim:main, :])
    gate_s[...] = gn_ref[vec, :] * (r * jax.nn.sigmoid(r))
    for c in range(n_chunks):
        rows = slice(c * CHUNK, (c + 1) * CHUNK)
        cum = cum_s[rows, :]
        tot = cum[CHUNK - 1:CHUNK, :]
        kd_s[rows, :] = k_s[rows, :] * jnp.exp(tot - cum)
        decay = jnp.exp(tot)
        for hd in range(GLA_HEADS):
            ks = slice(hd * head_k, (hd + 1) * head_k)
            dcol_s[c, hd] = jnp.broadcast_to(decay[:, ks], (head_k, head_k)).T
    kdt_s[...] = kd_s[...].T
    lane = lax.broadcasted_iota(jnp.int32, (head_k, LANES), 1)
    for c in range(n_chunks):
        blk = slice((c * CHUNK) // LANES * LANES, ((c * CHUNK) // LANES + 1) * LANES)
        lo = (c * CHUNK) % LANES
        if lo == 0:
            in_chunk = lane < CHUNK
        elif lo + CHUNK == LANES:
            in_chunk = lane >= lo
        else:
            in_chunk = (lane >= lo) & (lane < lo + CHUNK)
        for hd in range(GLA_HEADS):
            ks = slice(hd * head_k, (hd + 1) * head_k)
            vs = slice(hd * head_v, (hd + 1) * head_v)
            k_dec_t = jnp.where(in_chunk, kdt_s[ks, blk], 0.0).astype(BF16)
            upd_s[c, hd] = _dot(k_dec_t, v_s[blk, vs])

    for hd in range(GLA_HEADS):
        ks = slice(hd * head_k, (hd + 1) * head_k)
        vs = slice(hd * head_v, (hd + 1) * head_v)
        state = state_s[hd]
        for c in range(n_chunks):
            rows = slice(c * CHUNK, (c + 1) * CHUNK)
            d_col = jnp.concatenate([dcol_s[c, hd]] * (head_v // head_k), axis=1)
            state = state * d_col + upd_s[c, hd]
            o_s[rows, vs] = _dot(q_s[rows, ks], state.astype(BF16))
        state_s[hd] = state

    parts = []
    for hd in range(GLA_HEADS):
        vs = slice(hd * head_v, (hd + 1) * head_v)
        o = o_s[:, vs]
        ms = jnp.mean(o * o, axis=-1, keepdims=True)
        parts.append((o * lax.rsqrt(ms + RMS_EPS) * gate_s[:, vs]).astype(BF16))
    y = jnp.concatenate(parts, axis=1)
    out_ref[0] = x + _dot(y, wo_ref[...])


def _sconv_kernel(x_ref, norm_ref, win_ref, cw_ref, wo_ref, out_ref, carry_s, *, layer, conv_dim):
    ts = x_ref.shape[1]

    @pl.when(pl.program_id(1) == 0)
    def _():
        carry_s[...] = jnp.zeros_like(carry_s)

    x = x_ref[0]
    h = _rmsnorm(x, norm_ref[layer:layer + 1, :]).astype(BF16)
    ch = _dot(h, win_ref[:, conv_dim:])
    z = ch[:, :conv_dim] * ch[:, conv_dim:]
    row_ids = lax.broadcasted_iota(jnp.int32, z.shape, 0)
    conv = _causal_conv3(z, carry_s[...], cw_ref[:, layer, :], row_ids)
    carry_s[...] = z[ts - SUBLANES:ts, :]
    b_gate = _dot(h, win_ref[:, 0:conv_dim])
    y = (b_gate * conv).astype(BF16)
    out_ref[0] = x + _dot(y, wo_ref[...])


def _ffn_kernel(x_ref, norm_ref, wup_ref, cw_ref, wdn_ref, fnorm_ref, out_ref, act_s, carry_s,
                *, layer, d_ff, final_norm):
    ts = x_ref.shape[1]

    @pl.when(pl.program_id(1) == 0)
    def _():
        carry_s[...] = jnp.zeros_like(carry_s)

    x = x_ref[0]
    h = _rmsnorm(x, norm_ref[layer:layer + 1, :]).astype(BF16)
    gu = _dot(h, wup_ref[...])
    g = gu[:, :d_ff]
    row_ids = lax.broadcasted_iota(jnp.int32, g.shape, 0)
    conv = _causal_conv3(g, carry_s[...], cw_ref[:, layer, :], row_ids)
    carry_s[...] = g[ts - SUBLANES:ts, :]
    act_s[...] = (conv * jax.nn.sigmoid(conv) * gu[:, d_ff:]).astype(BF16)
    y = x + _dot(act_s[...], wdn_ref[...])
    if final_norm:
        y = _rmsnorm(y, fnorm_ref[...])
    out_ref[0] = y


def _nbytes(shape, dtype):
    n = 1
    for s in shape:
        n *= s
    return n * jnp.dtype(dtype).itemsize


def _layer_param(stacked, layer):
    spec = pl.BlockSpec((None,) + stacked.shape[1:], lambda b, t: (layer, 0, 0),
                        pipeline_mode=pl.Buffered(1))
    return stacked, spec, _nbytes(stacked.shape[1:], stacked.dtype)


def _whole_param(arr):
    spec = pl.BlockSpec(arr.shape, lambda b, t: (0,) * arr.ndim, pipeline_mode=pl.Buffered(1))
    return arr, spec, _nbytes(arr.shape, arr.dtype)


def _conv_param(conv_w):
    return _whole_param(jnp.swapaxes(conv_w, 0, 1))


def _call(kernel, x, params, scratch, name, temp_bytes, ts=SEQ_TILE):
    bsz, seq, d = x.shape
    assert seq % ts == 0 and ts % LANES == 0 and LANES % CHUNK == 0
    tile = pl.BlockSpec((1, ts, d), lambda b, t: (b, t, 0))
    arrays = [p[0] for p in params]
    resident = sum(p[2] for p in params)
    scratch_bytes = sum(_nbytes(s.shape, s.dtype) for s in scratch)
    streamed = 2 * 2 * _nbytes((ts, d), F32)
    vmem = resident + scratch_bytes + streamed + temp_bytes
    assert vmem < VMEM_BYTES_V7X, vmem
    return pl.pallas_call(
        kernel,
        out_shape=jax.ShapeDtypeStruct(x.shape, x.dtype),
        grid=(bsz, seq // ts),
        in_specs=[tile] + [p[1] for p in params],
        out_specs=tile,
        scratch_shapes=scratch,
        compiler_params=pltpu.CompilerParams(
            dimension_semantics=("arbitrary", "arbitrary"), vmem_limit_bytes=vmem),
        name=name,
    )(x, *arrays)


def _gla_layer(x, j, norm, w_in, w_gate_up, b_gate, gn, w_out):
    key_dim = w_gate_up.shape[-1]
    value_dim = gn.shape[-1]
    head_k, head_v = key_dim // GLA_HEADS, value_dim // GLA_HEADS
    ts = SEQ_TILE
    w_in_t = jnp.swapaxes(w_in, 1, 2)
    params = [_whole_param(norm), _layer_param(w_in_t, j), _layer_param(w_gate_up, j),
              _whole_param(b_gate), _whole_param(gn), _layer_param(w_out, j)]
    scratch = [
        pltpu.VMEM((ts, x.shape[-1]), BF16),
        pltpu.VMEM((ts, key_dim), BF16),
        pltpu.VMEM((ts, key_dim), F32),
        pltpu.VMEM((ts, value_dim), BF16),
        pltpu.VMEM((ts, key_dim), F32),
        pltpu.VMEM((ts, key_dim), F32),
        pltpu.VMEM((ts, key_dim), F32),
        pltpu.VMEM((key_dim, ts), F32),
        pltpu.VMEM((ts, value_dim), F32),
        pltpu.VMEM((ts, value_dim), F32),
        pltpu.VMEM((ts // CHUNK, GLA_HEADS, head_k, head_v), F32),
        pltpu.VMEM((ts // CHUNK, GLA_HEADS, head_k, head_k), F32),
        pltpu.VMEM((GLA_HEADS, head_k, head_v), F32),
    ]
    temp = 6 * _nbytes((ts, value_dim), F32)
    kern = functools.partial(_gla_kernel, layer=j, key_dim=key_dim, value_dim=value_dim)
    return _call(kern, x, params, scratch, "gla_layer", temp)


def _sconv_layer(x, j, norm, w_in, conv_w, w_out):
    conv_dim = conv_w.shape[-1]
    params = [_whole_param(norm), _layer_param(w_in, j), _conv_param(conv_w), _layer_param(w_out, j)]
    scratch = [pltpu.VMEM((SUBLANES, conv_dim), F32)]
    temp = 6 * _nbytes((SCONV_SEQ_TILE, conv_dim), F32)
    kern = functools.partial(_sconv_kernel, layer=j, conv_dim=conv_dim)
    return _call(kern, x, params, scratch, "sconv_layer", temp, ts=SCONV_SEQ_TILE)


def _ffn_layer(x, i, norm, w_up, conv_w, w_down, final_norm_w, final_norm):
    d = x.shape[-1]
    d_ff = conv_w.shape[-1]
    params = [_whole_param(norm), _layer_param(w_up, i), _conv_param(conv_w), _layer_param(w_down, i),
              _whole_param(final_norm_w.reshape(1, d))]
    scratch = [
        pltpu.VMEM((SEQ_TILE, d_ff), BF16),
        pltpu.VMEM((SUBLANES, d_ff), F32),
    ]
    temp = 6 * _nbytes((SEQ_TILE, d), F32)
    kern = functools.partial(_ffn_kernel, layer=i, d_ff=d_ff, final_norm=final_norm)
    return _call(kern, x, params, scratch, "ffn_final" if final_norm else "ffn_layer", temp)


def kernel(x, a_norm, a_w_in, a_w_gate_up, a_b_gate, a_gn, a_w_out, b_norm, b_w_in, b_conv, b_w_out,
           f_norm, f_w_up, f_conv, f_w_down, final_norm):
    depth = f_norm.shape[0]
    for i in range(depth):
        j = i // N_MIXERS
        if i % N_MIXERS == 0:
            x = _gla_layer(x, j, a_norm, a_w_in, a_w_gate_up, a_b_gate, a_gn, a_w_out)
        else:
            x = _sconv_layer(x, j, b_norm, b_w_in, b_conv, b_w_out)
        x = _ffn_layer(x, i, f_norm, f_w_up, f_conv, f_w_down, final_norm,
                       final_norm=(i == depth - 1))
    return x
```

```python
import functools

import jax
import jax.numpy as jnp
from jax import lax
from jax.experimental import pallas as pl
from jax.experimental.pallas import tpu as pltpu

F32 = jnp.float32
BF16 = jnp.bfloat16

CHUNK = 64
GLA_HEADS = 4
GATE_NORMALIZER = 16.0
RMS_EPS = 1e-6
N_MIXERS = 2

SUBLANES = 8
BF16_SUBLANES = 16
LANES = 128
VMEM_BYTES_V7X = 64 * 1024 * 1024

SEQ_TILE = 512
SCONV_SEQ_TILE = 1024


def _dot(a, b):
    return lax.dot_general(a, b, (((1,), (0,)), ((), ())), preferred_element_type=F32)


def _dot_nt(a, bt):
    return lax.dot_general(a, bt, (((1,), (1,)), ((), ())), preferred_element_type=F32)


def _rmsnorm(x, g):
    ms = jnp.mean(x * x, axis=-1, keepdims=True)
    return x * lax.rsqrt(ms + RMS_EPS) * g


def _causal_conv3(z, prev, w, row_ids):
    zm1 = jnp.where(row_ids == 0, prev[7:8, :], pltpu.roll(z, 1, 0))
    zm2 = jnp.where(row_ids == 0, prev[6:7, :],
                    jnp.where(row_ids == 1, prev[7:8, :], pltpu.roll(z, 2, 0)))
    return w[0:1, :] * zm2 + w[1:2, :] * zm1 + w[2:3, :] * z


def _log_sigmoid(x):
    return jnp.minimum(x, 0.0) - jnp.log(1.0 + jnp.exp(-jnp.abs(x)))


def _gla_kernel(x_ref, norm_ref, wint_ref, wgu_ref, bg_ref, gn_ref, wo_ref, out_ref,
                h_s, q_s, k_s, v_s, la_s, cum_s, kd_s, kdt_s, gate_s, o_s, upd_s, dcol_s, state_s,
                *, layer, key_dim, value_dim):
    ts = x_ref.shape[1]
    vec = slice(layer, layer + 1)
    n_chunks = ts // CHUNK
    head_k = key_dim // GLA_HEADS
    head_v = value_dim // GLA_HEADS
    main = 2 * key_dim + 2 * value_dim

    @pl.when(pl.program_id(1) == 0)
    def _():
        state_s[...] = jnp.zeros_like(state_s)

    x = x_ref[0]
    h_s[...] = _rmsnorm(x, norm_ref[vec, :]).astype(BF16)
    gl = _dot_nt(h_s[...], wint_ref[main:, :])
    qk = _dot_nt(h_s[...], wint_ref[0:2 * key_dim, :])
    q_s[...] = (qk[:, :key_dim] * (head_k ** -0.5)).astype(BF16)
    k_s[...] = qk[:, key_dim:]
    logits = _dot(gl.astype(BF16), wgu_ref[...]) + bg_ref[vec, :]
    la_s[...] = _log_sigmoid(logits) * (1.0 / GATE_NORMALIZER)
    v_s[...] = _dot_nt(h_s[...], wint_ref[2 * key_dim:2 * key_dim + value_dim, :]).astype(BF16)

    r_i = lax.broadcasted_iota(jnp.int32, (CHUNK, CHUNK), 0)
    c_i = lax.broadcasted_iota(jnp.int32, (CHUNK, CHUNK), 1)
    tri = jnp.where(r_i >= c_i, 1.0, 0.0).astype(BF16)

    for c in range(n_chunks):
        rows = slice(c * CHUNK, (c + 1) * CHUNK)
        la = la_s[rows, :]
        la_hi = la.astype(BF16)
        la_lo = (la - la_hi.astype(F32)).astype(BF16)
        cum_s[rows, :] = _dot(tri, la_hi) + _dot(tri, la_lo)
    r = _dot_nt(h_s[...], wint_ref[2 * key_dim + value_dim:main, :])
    gate_s[...] = gn_ref[vec, :] * (r * jax.nn.sigmoid(r))
    for c in range(n_chunks):
        rows = slice(c * CHUNK, (c + 1) * CHUNK)
        cum = cum_s[rows, :]
        tot = cum[CHUNK - 1:CHUNK, :]
        kd_s[rows, :] = k_s[rows, :] * jnp.exp(tot - cum)
        decay = jnp.exp(tot)
        for hd in range(GLA_HEADS):
            ks = slice(hd * head_k, (hd + 1) * head_k)
            dcol_s[c, hd] = jnp.broadcast_to(decay[:, ks], (head_k, head_k)).T
    kdt_s[...] = kd_s[...].T
    lane = lax.broadcasted_iota(jnp.int32, (head_k, LANES), 1)
    for c in range(n_chunks):
        blk = slice((c * CHUNK) // LANES * LANES, ((c * CHUNK) // LANES + 1) * LANES)
        lo = (c * CHUNK) % LANES
        if lo == 0:
            in_chunk = lane < CHUNK
        elif lo + CHUNK == LANES:
            in_chunk = lane >= lo
        else:
            in_chunk = (lane >= lo) & (lane < lo + CHUNK)
        for hd in range(GLA_HEADS):
            ks = slice(hd * head_k, (hd + 1) * head_k)
            vs = slice(hd * head_v, (hd + 1) * head_v)
            k_dec_t = jnp.where(in_chunk, kdt_s[ks, blk], 0.0).astype(BF16)
            upd_s[c, hd] = _dot(k_dec_t, v_s[blk, vs])

    for hd in range(GLA_HEADS):
        ks = slice(hd * head_k, (hd + 1) * head_k)
        vs = slice(hd * head_v, (hd + 1) * head_v)
        state = state_s[hd]
        for c in range(n_chunks):
            rows = slice(c * CHUNK, (c + 1) * CHUNK)
            d_col = jnp.concatenate([dcol_s[c, hd]] * (head_v // head_k), axis=1)
            state = state * d_col + upd_s[c, hd]
            o_s[rows, vs] = _dot(q_s[rows, ks], state.astype(BF16))
        state_s[hd] = state

    parts = []
    for hd in range(GLA_HEADS):
        vs = slice(hd * head_v, (hd + 1) * head_v)
        o = o_s[:, vs]
        ms = jnp.mean(o * o, axis=-1, keepdims=True)
        parts.append((o * lax.rsqrt(ms + RMS_EPS) * gate_s[:, vs]).astype(BF16))
    y = jnp.concatenate(parts, axis=1)
    out_ref[0] = x + _dot(y, wo_ref[...])


def _sconv_kernel(x_ref, norm_ref, win_ref, cw_ref, wo_ref, out_ref, carry_s, *, layer, conv_dim):
    ts = x_ref.shape[1]

    @pl.when(pl.program_id(1) == 0)
    def _():
        carry_s[...] = jnp.zeros_like(carry_s)

    x = x_ref[0]
    h = _rmsnorm(x, norm_ref[layer:layer + 1, :]).astype(BF16)
    ch = _dot(h, win_ref[:, conv_dim:])
    z = ch[:, :conv_dim] * ch[:, conv_dim:]
    row_ids = lax.broadcasted_iota(jnp.int32, z.shape, 0)
    conv = _causal_conv3(z, carry_s[...], cw_ref[:, layer, :], row_ids)
    carry_s[...] = z[ts - SUBLANES:ts, :]
    b_gate = _dot(h, win_ref[:, 0:conv_dim])
    y = (b_gate * conv).astype(BF16)
    out_ref[0] = x + _dot(y, wo_ref[...])


def _ffn_kernel(x_ref, norm_ref, wup_ref, cw_ref, wdn_ref, fnorm_ref, out_ref, act_s, carry_s,
                *, layer, d_ff, final_norm):
    ts = x_ref.shape[1]

    @pl.when(pl.program_id(1) == 0)
    def _():
        carry_s[...] = jnp.zeros_like(carry_s)

    x = x_ref[0]
    h = _rmsnorm(x, norm_ref[layer:layer + 1, :]).astype(BF16)
    gu = _dot(h, wup_ref[...])
    g = gu[:, :d_ff]
    row_ids = lax.broadcasted_iota(jnp.int32, g.shape, 0)
    conv = _causal_conv3(g, carry_s[...], cw_ref[:, layer, :], row_ids)
    carry_s[...] = g[ts - SUBLANES:ts, :]
    act_s[...] = (conv * jax.nn.sigmoid(conv) * gu[:, d_ff:]).astype(BF16)
    y = x + _dot(act_s[...], wdn_ref[...])
    if final_norm:
        y = _rmsnorm(y, fnorm_ref[...])
    out_ref[0] = y


def _nbytes(shape, dtype):
    n = 1
    for s in shape:
        n *= s
    return n * jnp.dtype(dtype).itemsize


def _layer_param(stacked, layer):
    spec = pl.BlockSpec((None,) + stacked.shape[1:], lambda b, t: (layer, 0, 0),
                        pipeline_mode=pl.Buffered(1))
    return stacked, spec, _nbytes(stacked.shape[1:], stacked.dtype)


def _whole_param(arr):
    spec = pl.BlockSpec(arr.shape, lambda b, t: (0,) * arr.ndim, pipeline_mode=pl.Buffered(1))
    return arr, spec, _nbytes(arr.shape, arr.dtype)


def _conv_param(conv_w):
    return _whole_param(jnp.swapaxes(conv_w, 0, 1))


def _weight_param(w, layer):
    return _layer_param(w, layer) if w.ndim == 3 else _whole_param(w)


def _cast_blocks(rows, steps):
    for n in range(steps, 0, -1):
        if steps % n == 0 and rows % n == 0 and (rows // n) % BF16_SUBLANES == 0:
            return n
    return None


def _with_casts(body, n_in, n_cast):
    def kernel(*refs):
        ins, srcs = refs[:n_in], refs[n_in:n_in + n_cast]
        out, dsts = refs[n_in + n_cast], refs[n_in + n_cast + 1:n_in + 2 * n_cast + 1]
        scratch = refs[n_in + 2 * n_cast + 1:]
        body(*ins, out, *scratch)
        for src, dst in zip(srcs, dsts):
            dst[...] = src[...].astype(dst.dtype)
    return kernel


def _call(body, x, params, scratch, name, temp_bytes, casts=(), ts=SEQ_TILE):
    bsz, seq, d = x.shape
    assert seq % ts == 0 and ts % LANES == 0 and LANES % CHUNK == 0
    n_t = seq // ts
    steps = bsz * n_t
    tile = pl.BlockSpec((1, ts, d), lambda b, t: (b, t, 0))
    arrays = [p[0] for p in params]
    in_specs = [tile] + [p[1] for p in params]
    out_shape = [jax.ShapeDtypeStruct(x.shape, x.dtype)]
    out_specs = [tile]
    streamed = 2 * 2 * _nbytes((ts, d), F32)
    for w, layer in casts:
        rows, cols = w.shape[1:]
        n_blk = _cast_blocks(rows, steps)
        rep = steps // n_blk
        arrays.append(w)
        in_specs.append(pl.BlockSpec((None, rows // n_blk, cols),
                                     lambda b, t, layer=layer, rep=rep: (layer, (b * n_t + t) // rep, 0)))
        out_shape.append(jax.ShapeDtypeStruct((rows, cols), BF16))
        out_specs.append(pl.BlockSpec((rows // n_blk, cols),
                                      lambda b, t, rep=rep: ((b * n_t + t) // rep, 0)))
        streamed += 2 * _nbytes((rows // n_blk, cols), F32) + 2 * _nbytes((rows // n_blk, cols), BF16)
    resident = sum(p[2] for p in params)
    scratch_bytes = sum(_nbytes(s.shape, s.dtype) for s in scratch)
    vmem = resident + scratch_bytes + streamed + temp_bytes
    assert vmem < VMEM_BYTES_V7X, vmem
    outs = pl.pallas_call(
        _with_casts(body, 1 + len(params), len(casts)),
        out_shape=out_shape,
        grid=(bsz, n_t),
        in_specs=in_specs,
        out_specs=out_specs,
        scratch_shapes=scratch,
        compiler_params=pltpu.CompilerParams(
            dimension_semantics=("arbitrary", "arbitrary"), vmem_limit_bytes=vmem),
        name=name,
    )(x, *arrays)
    return outs[0], list(outs[1:])


def _castable(w, layer, x, ts):
    steps = x.shape[0] * (x.shape[1] // ts)
    return (w, layer) if _cast_blocks(w.shape[1], steps) else None


def _gla_layer(x, j, norm, w_in, w_gate_up, b_gate, gn, w_out, casts):
    key_dim = w_gate_up.shape[-1]
    value_dim = gn.shape[-1]
    head_k, head_v = key_dim // GLA_HEADS, value_dim // GLA_HEADS
    ts = SEQ_TILE
    w_in_t = jnp.swapaxes(w_in, 1, 2)
    params = [_whole_param(norm), _layer_param(w_in_t, j), _layer_param(w_gate_up, j),
              _whole_param(b_gate), _whole_param(gn), _layer_param(w_out, j)]
    scratch = [
        pltpu.VMEM((ts, x.shape[-1]), BF16),
        pltpu.VMEM((ts, key_dim), BF16),
        pltpu.VMEM((ts, key_dim), F32),
        pltpu.VMEM((ts, value_dim), BF16),
        pltpu.VMEM((ts, key_dim), F32),
        pltpu.VMEM((ts, key_dim), F32),
        pltpu.VMEM((ts, key_dim), F32),
        pltpu.VMEM((key_dim, ts), F32),
        pltpu.VMEM((ts, value_dim), F32),
        pltpu.VMEM((ts, value_dim), F32),
        pltpu.VMEM((ts // CHUNK, GLA_HEADS, head_k, head_v), F32),
        pltpu.VMEM((ts // CHUNK, GLA_HEADS, head_k, head_k), F32),
        pltpu.VMEM((GLA_HEADS, head_k, head_v), F32),
    ]
    temp = 6 * _nbytes((ts, value_dim), F32)
    kern = functools.partial(_gla_kernel, layer=j, key_dim=key_dim, value_dim=value_dim)
    return _call(kern, x, params, scratch, "gla_layer", temp, casts)


def _sconv_layer(x, j, norm, w_in, conv_w, w_out, casts):
    conv_dim = conv_w.shape[-1]
    params = [_whole_param(norm), _weight_param(w_in, j), _conv_param(conv_w), _weight_param(w_out, j)]
    scratch = [pltpu.VMEM((SUBLANES, conv_dim), F32)]
    temp = 6 * _nbytes((SCONV_SEQ_TILE, conv_dim), F32)
    kern = functools.partial(_sconv_kernel, layer=j, conv_dim=conv_dim)
    return _call(kern, x, params, scratch, "sconv_layer", temp, casts, ts=SCONV_SEQ_TILE)


def _ffn_layer(x, i, norm, w_up, conv_w, w_down, final_norm_w, final_norm, casts):
    d = x.shape[-1]
    d_ff = conv_w.shape[-1]
    params = [_whole_param(norm), _weight_param(w_up, i), _conv_param(conv_w), _weight_param(w_down, i),
              _whole_param(final_norm_w.reshape(1, d))]
    scratch = [
        pltpu.VMEM((SEQ_TILE, d_ff), BF16),
        pltpu.VMEM((SUBLANES, d_ff), F32),
    ]
    temp = 6 * _nbytes((SEQ_TILE, d), F32)
    kern = functools.partial(_ffn_kernel, layer=i, d_ff=d_ff, final_norm=final_norm)
    return _call(kern, x, params, scratch, "ffn_final" if final_norm else "ffn_layer", temp, casts)


def kernel(x, a_norm, a_w_in, a_w_gate_up, a_b_gate, a_gn, a_w_out, b_norm, b_w_in, b_conv, b_w_out,
           f_norm, f_w_up, f_conv, f_w_down, final_norm):
    depth = f_norm.shape[0]
    plan = []
    for i in range(depth):
        plan.append(("gla" if i % N_MIXERS == 0 else "sconv", i // N_MIXERS))
        plan.append(("ffn", i))
    tile_of = {"gla": SEQ_TILE, "ffn": SEQ_TILE, "sconv": SCONV_SEQ_TILE}
    weights_of = {"gla": (), "sconv": (b_w_in, b_w_out), "ffn": (f_w_up, f_w_down)}

    ready = {}
    for pos, (kind, idx) in enumerate(plan):
        casts = []
        if pos + 1 < len(plan):
            nxt_kind, nxt_idx = plan[pos + 1]
            casts = [_castable(w, nxt_idx, x, tile_of[kind]) for w in weights_of[nxt_kind]]
            casts = casts if all(c is not None for c in casts) else []
        mats = ready.get(pos, weights_of[kind])
        if kind == "gla":
            x, done = _gla_layer(x, idx, a_norm, a_w_in, a_w_gate_up, a_b_gate, a_gn, a_w_out, casts)
        elif kind == "sconv":
            x, done = _sconv_layer(x, idx, b_norm, mats[0], b_conv, mats[1], casts)
        else:
            x, done = _ffn_layer(x, idx, f_norm, mats[0], f_conv, mats[1], final_norm,
                                 final_norm=(pos == len(plan) - 1), casts=casts)
        if done:
            ready[pos + 1] = done
    return x
```

```python
import functools

import jax
import jax.numpy as jnp
from jax import lax
from jax.experimental import pallas as pl
from jax.experimental.pallas import tpu as pltpu

F32 = jnp.float32
BF16 = jnp.bfloat16

CHUNK = 64
GLA_HEADS = 4
GATE_NORMALIZER = 16.0
RMS_EPS = 1e-6
N_MIXERS = 2
CONV_WIDTH = 3

SUBLANES = 8
BF16_SUBLANES = 16
LANES = 128
VMEM_BYTES_V7X = 64 * 1024 * 1024

SEQ_TILE = 512
SCONV_SEQ_TILE = 1024


def _dot(a, b):
    return lax.dot_general(a, b, (((1,), (0,)), ((), ())), preferred_element_type=F32)


def _dot_nt(a, bt):
    return lax.dot_general(a, bt, (((1,), (1,)), ((), ())), preferred_element_type=F32)


def _rmsnorm(x, g):
    ms = jnp.mean(x * x, axis=-1, keepdims=True)
    return x * lax.rsqrt(ms + RMS_EPS) * g


def _causal_conv3(z, prev, w, row_ids):
    assert w.shape[0] == CONV_WIDTH
    last, before = prev[SUBLANES - 1:SUBLANES, :], prev[SUBLANES - 2:SUBLANES - 1, :]
    zm1 = jnp.where(row_ids == 0, last, pltpu.roll(z, 1, 0))
    zm2 = jnp.where(row_ids == 0, before, jnp.where(row_ids == 1, last, pltpu.roll(z, 2, 0)))
    return w[0:1, :] * zm2 + w[1:2, :] * zm1 + w[2:3, :] * z


def _log_sigmoid(x):
    return jnp.minimum(x, 0.0) - jnp.log(1.0 + jnp.exp(-jnp.abs(x)))


def _gla_kernel(x_ref, norm_ref, wint_ref, wgu_ref, bg_ref, gn_ref, wo_ref, out_ref,
                h_s, q_s, k_s, v_s, la_s, cum_s, kd_s, kdt_s, gate_s, o_s, upd_s, dcol_s, state_s,
                *, layer, key_dim, value_dim):
    ts = x_ref.shape[1]
    vec = slice(layer, layer + 1)
    n_chunks = ts // CHUNK
    head_k = key_dim // GLA_HEADS
    head_v = value_dim // GLA_HEADS
    main = 2 * key_dim + 2 * value_dim

    @pl.when(pl.program_id(1) == 0)
    def _():
        state_s[...] = jnp.zeros_like(state_s)

    x = x_ref[0]
    h_s[...] = _rmsnorm(x, norm_ref[vec, :]).astype(BF16)
    gl = _dot_nt(h_s[...], wint_ref[main:, :])
    qk = _dot_nt(h_s[...], wint_ref[0:2 * key_dim, :])
    q_s[...] = (qk[:, :key_dim] * (head_k ** -0.5)).astype(BF16)
    k_s[...] = qk[:, key_dim:]
    logits = _dot(gl.astype(BF16), wgu_ref[...]) + bg_ref[vec, :]
    la_s[...] = _log_sigmoid(logits) * (1.0 / GATE_NORMALIZER)
    v_s[...] = _dot_nt(h_s[...], wint_ref[2 * key_dim:2 * key_dim + value_dim, :]).astype(BF16)

    r_i = lax.broadcasted_iota(jnp.int32, (CHUNK, CHUNK), 0)
    c_i = lax.broadcasted_iota(jnp.int32, (CHUNK, CHUNK), 1)
    tri = jnp.where(r_i >= c_i, 1.0, 0.0).astype(BF16)

    for c in range(n_chunks):
        rows = slice(c * CHUNK, (c + 1) * CHUNK)
        la = la_s[rows, :]
        la_hi = la.astype(BF16)
        la_lo = (la - la_hi.astype(F32)).astype(BF16)
        cum_s[rows, :] = _dot(tri, la_hi) + _dot(tri, la_lo)
    r = _dot_nt(h_s[...], wint_ref[2 * key_dim + value_dim:main, :])
    gate_s[...] = gn_ref[vec, :] * (r * jax.nn.sigmoid(r))
    for c in range(n_chunks):
        rows = slice(c * CHUNK, (c + 1) * CHUNK)
        cum = cum_s[rows, :]
        tot = cum[CHUNK - 1:CHUNK, :]
        kd_s[rows, :] = k_s[rows, :] * jnp.exp(tot - cum)
        decay = jnp.exp(tot)
        for hd in range(GLA_HEADS):
            ks = slice(hd * head_k, (hd + 1) * head_k)
            dcol_s[c, hd] = jnp.broadcast_to(decay[:, ks], (head_k, head_k)).T
    kdt_s[...] = kd_s[...].T
    lane = lax.broadcasted_iota(jnp.int32, (head_k, LANES), 1)
    for c in range(n_chunks):
        blk = slice((c * CHUNK) // LANES * LANES, ((c * CHUNK) // LANES + 1) * LANES)
        lo = (c * CHUNK) % LANES
        if lo == 0:
            in_chunk = lane < CHUNK
        elif lo + CHUNK == LANES:
            in_chunk = lane >= lo
        else:
            in_chunk = (lane >= lo) & (lane < lo + CHUNK)
        for hd in range(GLA_HEADS):
            ks = slice(hd * head_k, (hd + 1) * head_k)
            vs = slice(hd * head_v, (hd + 1) * head_v)
            k_dec_t = jnp.where(in_chunk, kdt_s[ks, blk], 0.0).astype(BF16)
            upd_s[c, hd] = _dot(k_dec_t, v_s[blk, vs])

    for hd in range(GLA_HEADS):
        ks = slice(hd * head_k, (hd + 1) * head_k)
        vs = slice(hd * head_v, (hd + 1) * head_v)
        state = state_s[hd]
        for c in range(n_chunks):
            rows = slice(c * CHUNK, (c + 1) * CHUNK)
            d_col = jnp.concatenate([dcol_s[c, hd]] * (head_v // head_k), axis=1)
            state = state * d_col + upd_s[c, hd]
            o_s[rows, vs] = _dot(q_s[rows, ks], state.astype(BF16))
        state_s[hd] = state

    parts = []
    for hd in range(GLA_HEADS):
        vs = slice(hd * head_v, (hd + 1) * head_v)
        o = o_s[:, vs]
        ms = jnp.mean(o * o, axis=-1, keepdims=True)
        parts.append((o * lax.rsqrt(ms + RMS_EPS) * gate_s[:, vs]).astype(BF16))
    y = jnp.concatenate(parts, axis=1)
    out_ref[0] = x + _dot(y, wo_ref[...])


def _sconv_kernel(x_ref, norm_ref, win_ref, cw_ref, wo_ref, out_ref, carry_s, *, layer, conv_dim):
    ts = x_ref.shape[1]

    @pl.when(pl.program_id(1) == 0)
    def _():
        carry_s[...] = jnp.zeros_like(carry_s)

    x = x_ref[0]
    h = _rmsnorm(x, norm_ref[layer:layer + 1, :]).astype(BF16)
    ch = _dot(h, win_ref[:, conv_dim:])
    z = ch[:, :conv_dim] * ch[:, conv_dim:]
    row_ids = lax.broadcasted_iota(jnp.int32, z.shape, 0)
    conv = _causal_conv3(z, carry_s[...], cw_ref[:, layer, :], row_ids)
    carry_s[...] = z[ts - SUBLANES:ts, :]
    b_gate = _dot(h, win_ref[:, 0:conv_dim])
    y = (b_gate * conv).astype(BF16)
    out_ref[0] = x + _dot(y, wo_ref[...])


def _ffn_kernel(x_ref, norm_ref, wup_ref, cw_ref, wdn_ref, fnorm_ref, out_ref, act_s, carry_s,
                *, layer, d_ff, final_norm):
    ts = x_ref.shape[1]

    @pl.when(pl.program_id(1) == 0)
    def _():
        carry_s[...] = jnp.zeros_like(carry_s)

    x = x_ref[0]
    h = _rmsnorm(x, norm_ref[layer:layer + 1, :]).astype(BF16)
    gu = _dot(h, wup_ref[...])
    g = gu[:, :d_ff]
    row_ids = lax.broadcasted_iota(jnp.int32, g.shape, 0)
    conv = _causal_conv3(g, carry_s[...], cw_ref[:, layer, :], row_ids)
    carry_s[...] = g[ts - SUBLANES:ts, :]
    act_s[...] = (conv * jax.nn.sigmoid(conv) * gu[:, d_ff:]).astype(BF16)
    y = x + _dot(act_s[...], wdn_ref[...])
    if final_norm:
        y = _rmsnorm(y, fnorm_ref[...])
    out_ref[0] = y


def _nbytes(shape, dtype):
    n = 1
    for s in shape:
        n *= s
    return n * jnp.dtype(dtype).itemsize


def _layer_param(stacked, layer):
    spec = pl.BlockSpec((None,) + stacked.shape[1:], lambda b, t: (layer, 0, 0),
                        pipeline_mode=pl.Buffered(1))
    return stacked, spec, _nbytes(stacked.shape[1:], stacked.dtype)


def _whole_param(arr):
    spec = pl.BlockSpec(arr.shape, lambda b, t: (0,) * arr.ndim, pipeline_mode=pl.Buffered(1))
    return arr, spec, _nbytes(arr.shape, arr.dtype)


def _conv_param(conv_w):
    return _whole_param(jnp.swapaxes(conv_w, 0, 1))


def _weight_param(w, layer):
    return _layer_param(w, layer) if w.ndim == 3 else _whole_param(w)


def _cast_blocks(rows, steps):
    for n in range(steps, 0, -1):
        if steps % n == 0 and rows % n == 0 and (rows // n) % BF16_SUBLANES == 0:
            return n
    return None


def _with_casts(body, n_in, n_cast):
    def kernel(*refs):
        ins, srcs = refs[:n_in], refs[n_in:n_in + n_cast]
        out, dsts = refs[n_in + n_cast], refs[n_in + n_cast + 1:n_in + 2 * n_cast + 1]
        scratch = refs[n_in + 2 * n_cast + 1:]
        body(*ins, out, *scratch)
        for src, dst in zip(srcs, dsts):
            dst[...] = src[...].astype(dst.dtype)
    return kernel


def _call(body, x, params, scratch, name, temp_bytes, casts=(), ts=SEQ_TILE):
    bsz, seq, d = x.shape
    assert seq % ts == 0 and ts % LANES == 0 and LANES % CHUNK == 0
    n_t = seq // ts
    steps = bsz * n_t
    tile = pl.BlockSpec((1, ts, d), lambda b, t: (b, t, 0))
    arrays = [p[0] for p in params]
    in_specs = [tile] + [p[1] for p in params]
    out_shape = [jax.ShapeDtypeStruct(x.shape, x.dtype)]
    out_specs = [tile]
    streamed = 2 * 2 * _nbytes((ts, d), F32)
    for w, layer in casts:
        rows, cols = w.shape[1:]
        n_blk = _cast_blocks(rows, steps)
        rep = steps // n_blk
        arrays.append(w)
        in_specs.append(pl.BlockSpec((None, rows // n_blk, cols),
                                     lambda b, t, layer=layer, rep=rep: (layer, (b * n_t + t) // rep, 0)))
        out_shape.append(jax.ShapeDtypeStruct((rows, cols), BF16))
        out_specs.append(pl.BlockSpec((rows // n_blk, cols),
                                      lambda b, t, rep=rep: ((b * n_t + t) // rep, 0)))
        streamed += 2 * _nbytes((rows // n_blk, cols), F32) + 2 * _nbytes((rows // n_blk, cols), BF16)
    resident = sum(p[2] for p in params)
    scratch_bytes = sum(_nbytes(s.shape, s.dtype) for s in scratch)
    vmem = resident + scratch_bytes + streamed + temp_bytes
    assert vmem < VMEM_BYTES_V7X, vmem
    outs = pl.pallas_call(
        _with_casts(body, 1 + len(params), len(casts)),
        out_shape=out_shape,
        grid=(bsz, n_t),
        in_specs=in_specs,
        out_specs=out_specs,
        scratch_shapes=scratch,
        compiler_params=pltpu.CompilerParams(
            dimension_semantics=("arbitrary", "arbitrary"), vmem_limit_bytes=vmem),
        name=name,
    )(x, *arrays)
    return outs[0], list(outs[1:])


def _castable(w, layer, x, ts):
    steps = x.shape[0] * (x.shape[1] // ts)
    return (w, layer) if _cast_blocks(w.shape[1], steps) else None


def _gla_layer(x, j, norm, w_in, w_gate_up, b_gate, gn, w_out, casts):
    key_dim = w_gate_up.shape[-1]
    value_dim = gn.shape[-1]
    head_k, head_v = key_dim // GLA_HEADS, value_dim // GLA_HEADS
    ts = SEQ_TILE
    w_in_t = jnp.swapaxes(w_in, 1, 2)
    params = [_whole_param(norm), _layer_param(w_in_t, j), _layer_param(w_gate_up, j),
              _whole_param(b_gate), _whole_param(gn), _layer_param(w_out, j)]
    scratch = [
        pltpu.VMEM((ts, x.shape[-1]), BF16),
        pltpu.VMEM((ts, key_dim), BF16),
        pltpu.VMEM((ts, key_dim), F32),
        pltpu.VMEM((ts, value_dim), BF16),
        pltpu.VMEM((ts, key_dim), F32),
        pltpu.VMEM((ts, key_dim), F32),
        pltpu.VMEM((ts, key_dim), F32),
        pltpu.VMEM((key_dim, ts), F32),
        pltpu.VMEM((ts, value_dim), F32),
        pltpu.VMEM((ts, value_dim), F32),
        pltpu.VMEM((ts // CHUNK, GLA_HEADS, head_k, head_v), F32),
        pltpu.VMEM((ts // CHUNK, GLA_HEADS, head_k, head_k), F32),
        pltpu.VMEM((GLA_HEADS, head_k, head_v), F32),
    ]
    temp = 6 * _nbytes((ts, value_dim), F32)
    kern = functools.partial(_gla_kernel, layer=j, key_dim=key_dim, value_dim=value_dim)
    return _call(kern, x, params, scratch, "gla_layer", temp, casts)


def _sconv_layer(x, j, norm, w_in, conv_w, w_out, casts):
    conv_dim = conv_w.shape[-1]
    params = [_whole_param(norm), _weight_param(w_in, j), _conv_param(conv_w), _weight_param(w_out, j)]
    scratch = [pltpu.VMEM((SUBLANES, conv_dim), F32)]
    temp = 6 * _nbytes((SCONV_SEQ_TILE, conv_dim), F32)
    kern = functools.partial(_sconv_kernel, layer=j, conv_dim=conv_dim)
    return _call(kern, x, params, scratch, "sconv_layer", temp, casts, ts=SCONV_SEQ_TILE)


def _ffn_layer(x, i, norm, w_up, conv_w, w_down, final_norm_w, final_norm, casts):
    d = x.shape[-1]
    d_ff = conv_w.shape[-1]
    params = [_whole_param(norm), _weight_param(w_up, i), _conv_param(conv_w), _weight_param(w_down, i),
              _whole_param(final_norm_w.reshape(1, d))]
    scratch = [
        pltpu.VMEM((SEQ_TILE, d_ff), BF16),
        pltpu.VMEM((SUBLANES, d_ff), F32),
    ]
    temp = 4 * _nbytes((SEQ_TILE, d_ff), F32) + 3 * _nbytes((SEQ_TILE, d), F32)
    kern = functools.partial(_ffn_kernel, layer=i, d_ff=d_ff, final_norm=final_norm)
    return _call(kern, x, params, scratch, "ffn_final" if final_norm else "ffn_layer", temp, casts)


def kernel(x, a_norm, a_w_in, a_w_gate_up, a_b_gate, a_gn, a_w_out, b_norm, b_w_in, b_conv, b_w_out,
           f_norm, f_w_up, f_conv, f_w_down, final_norm):
    depth = f_norm.shape[0]
    plan = []
    for i in range(depth):
        plan.append(("gla" if i % N_MIXERS == 0 else "sconv", i // N_MIXERS))
        plan.append(("ffn", i))
    tile_of = {"gla": SEQ_TILE, "ffn": SEQ_TILE, "sconv": SCONV_SEQ_TILE}
    weights_of = {"gla": (), "sconv": (b_w_in, b_w_out), "ffn": (f_w_up, f_w_down)}

    ready = {}
    for pos, (kind, idx) in enumerate(plan):
        casts = []
        if pos + 1 < len(plan):
            nxt_kind, nxt_idx = plan[pos + 1]
            casts = [_castable(w, nxt_idx, x, tile_of[kind]) for w in weights_of[nxt_kind]]
            casts = casts if all(c is not None for c in casts) else []
        mats = ready.get(pos, weights_of[kind])
        if kind == "gla":
            x, done = _gla_layer(x, idx, a_norm, a_w_in, a_w_gate_up, a_b_gate, a_gn, a_w_out, casts)
        elif kind == "sconv":
            x, done = _sconv_layer(x, idx, b_norm, mats[0], b_conv, mats[1], casts)
        else:
            x, done = _ffn_layer(x, idx, f_norm, mats[0], f_conv, mats[1], final_norm,
                                 final_norm=(pos == len(plan) - 1), casts=casts)
        if done:
            ready[pos + 1] = done
    return x
```

```python
import functools

import jax
import jax.numpy as jnp
from jax import lax
from jax.experimental import pallas as pl
from jax.experimental.pallas import tpu as pltpu

F32 = jnp.float32
BF16 = jnp.bfloat16

CHUNK = 64
GLA_HEADS = 4
GATE_NORMALIZER = 16.0
RMS_EPS = 1e-6
N_MIXERS = 2
CONV_WIDTH = 3

SUBLANES = 8
BF16_SUBLANES = 16
LANES = 128
SCOPED_VMEM_CAP_BYTES = 60000 * 1024

SEQ_TILE = 512
SCONV_SEQ_TILE = 1024
FFN_SEQ_TILE = 1024
FFN_ROW_BLOCK = 256
GLA_SEQ_TILE = 1024


def _dot(a, b):
    return lax.dot_general(a, b, (((1,), (0,)), ((), ())), preferred_element_type=F32)


def _dot_nt(a, bt):
    return lax.dot_general(a, bt, (((1,), (1,)), ((), ())), preferred_element_type=F32)


def _rmsnorm(x, g):
    ms = jnp.mean(x * x, axis=-1, keepdims=True)
    return x * lax.rsqrt(ms + RMS_EPS) * g


def _causal_conv3(z, prev, w, row_ids):
    assert w.shape[0] == CONV_WIDTH
    last, before = prev[SUBLANES - 1:SUBLANES, :], prev[SUBLANES - 2:SUBLANES - 1, :]
    zm1 = jnp.where(row_ids == 0, last, pltpu.roll(z, 1, 0))
    zm2 = jnp.where(row_ids == 0, before, jnp.where(row_ids == 1, last, pltpu.roll(z, 2, 0)))
    return w[0:1, :] * zm2 + w[1:2, :] * zm1 + w[2:3, :] * z


def _log_sigmoid(x):
    return jnp.minimum(x, 0.0) - jnp.log(1.0 + jnp.exp(-jnp.abs(x)))


def _gla_kernel(x_ref, norm_ref, wint_ref, wgu_ref, bg_ref, gn_ref, wo_ref, out_ref, *scratch, **static):
    state_s = scratch[-1]
    sub = scratch[0].shape[0]

    @pl.when(pl.program_id(1) == 0)
    def _():
        state_s[...] = jnp.zeros_like(state_s)

    for r0 in range(0, x_ref.shape[1], sub):
        rows = slice(r0, r0 + sub)
        out_ref[0, rows, :] = _gla_block(x_ref[0, rows, :], norm_ref, wint_ref, wgu_ref, bg_ref, gn_ref,
                                         wo_ref, *scratch, **static)


def _gla_block(x, norm_ref, wint_ref, wgu_ref, bg_ref, gn_ref, wo_ref,
               h_s, q_s, k_s, v_s, la_s, cum_s, kd_s, kdt_s, gate_s, o_s, dcol_s, state_s,
               *, layer, key_dim, value_dim):
    ts = x.shape[0]
    vec = slice(layer, layer + 1)
    n_chunks = ts // CHUNK
    head_k = key_dim // GLA_HEADS
    head_v = value_dim // GLA_HEADS
    main = 2 * key_dim + 2 * value_dim

    h_s[...] = _rmsnorm(x, norm_ref[vec, :]).astype(BF16)
    gl = _dot_nt(h_s[...], wint_ref[main:, :])
    qk = _dot_nt(h_s[...], wint_ref[0:2 * key_dim, :])
    q_s[...] = (qk[:, :key_dim] * (head_k ** -0.5)).astype(BF16)
    k_s[...] = qk[:, key_dim:]
    logits = _dot(gl.astype(BF16), wgu_ref[...]) + bg_ref[vec, :]
    la_s[...] = _log_sigmoid(logits) * (1.0 / GATE_NORMALIZER)
    v_s[...] = _dot_nt(h_s[...], wint_ref[2 * key_dim:2 * key_dim + value_dim, :]).astype(BF16)

    r_i = lax.broadcasted_iota(jnp.int32, (CHUNK, CHUNK), 0)
    c_i = lax.broadcasted_iota(jnp.int32, (CHUNK, CHUNK), 1)
    tri = jnp.where(r_i >= c_i, 1.0, 0.0).astype(BF16)

    for c in range(n_chunks):
        rows = slice(c * CHUNK, (c + 1) * CHUNK)
        la = la_s[rows, :]
        la_hi = la.astype(BF16)
        la_lo = (la - la_hi.astype(F32)).astype(BF16)
        cum_s[rows, :] = _dot(tri, la_hi) + _dot(tri, la_lo)
    r = _dot_nt(h_s[...], wint_ref[2 * key_dim + value_dim:main, :])
    gate_s[...] = gn_ref[vec, :] * (r * jax.nn.sigmoid(r))
    for c in range(n_chunks):
        rows = slice(c * CHUNK, (c + 1) * CHUNK)
        cum = cum_s[rows, :]
        tot = cum[CHUNK - 1:CHUNK, :]
        kd_s[rows, :] = k_s[rows, :] * jnp.exp(tot - cum)
        decay = jnp.exp(tot)
        for hd in range(GLA_HEADS):
            ks = slice(hd * head_k, (hd + 1) * head_k)
            dcol_s[c, hd] = jnp.broadcast_to(decay[:, ks], (head_k, head_k)).T
    kdt_s[...] = kd_s[...].T
    lane = lax.broadcasted_iota(jnp.int32, (head_k, LANES), 1)

    def update(c, hd):
        blk = slice((c * CHUNK) // LANES * LANES, ((c * CHUNK) // LANES + 1) * LANES)
        lo = (c * CHUNK) % LANES
        if lo == 0:
            in_chunk = lane < CHUNK
        elif lo + CHUNK == LANES:
            in_chunk = lane >= lo
        else:
            in_chunk = (lane >= lo) & (lane < lo + CHUNK)
        ks = slice(hd * head_k, (hd + 1) * head_k)
        vs = slice(hd * head_v, (hd + 1) * head_v)
        k_dec_t = jnp.where(in_chunk, kdt_s[ks, blk], 0.0).astype(BF16)
        return _dot(k_dec_t, v_s[blk, vs])

    states = [state_s[hd] for hd in range(GLA_HEADS)]
    nxt = [update(0, hd) for hd in range(GLA_HEADS)]
    for c in range(n_chunks):
        rows = slice(c * CHUNK, (c + 1) * CHUNK)
        cur = nxt
        if c + 1 < n_chunks:
            nxt = [update(c + 1, hd) for hd in range(GLA_HEADS)]
        for hd in range(GLA_HEADS):
            ks = slice(hd * head_k, (hd + 1) * head_k)
            vs = slice(hd * head_v, (hd + 1) * head_v)
            d_col = jnp.concatenate([dcol_s[c, hd]] * (head_v // head_k), axis=1)
            states[hd] = states[hd] * d_col + cur[hd]
            o_s[rows, vs] = _dot(q_s[rows, ks], states[hd].astype(BF16))
    for hd in range(GLA_HEADS):
        state_s[hd] = states[hd]

    parts = []
    for hd in range(GLA_HEADS):
        vs = slice(hd * head_v, (hd + 1) * head_v)
        o = o_s[:, vs]
        ms = jnp.mean(o * o, axis=-1, keepdims=True)
        parts.append((o * lax.rsqrt(ms + RMS_EPS) * gate_s[:, vs]).astype(BF16))
    y = jnp.concatenate(parts, axis=1)
    return x + _dot(y, wo_ref[...])


def _sconv_kernel(x_ref, norm_ref, win_ref, cw_ref, wo_ref, out_ref, carry_s, *, layer, conv_dim):
    ts = x_ref.shape[1]

    @pl.when(pl.program_id(1) == 0)
    def _():
        carry_s[...] = jnp.zeros_like(carry_s)

    x = x_ref[0]
    h = _rmsnorm(x, norm_ref[layer:layer + 1, :]).astype(BF16)
    ch = _dot(h, win_ref[:, conv_dim:])
    z = ch[:, :conv_dim] * ch[:, conv_dim:]
    row_ids = lax.broadcasted_iota(jnp.int32, z.shape, 0)
    conv = _causal_conv3(z, carry_s[...], cw_ref[:, layer, :], row_ids)
    carry_s[...] = z[ts - SUBLANES:ts, :]
    b_gate = _dot(h, win_ref[:, 0:conv_dim])
    y = (b_gate * conv).astype(BF16)
    out_ref[0] = x + _dot(y, wo_ref[...])


def _ffn_kernel(x_ref, norm_ref, wup_ref, cw_ref, wdn_ref, fnorm_ref, out_ref, act_s, carry_s,
                *, layer, d_ff, final_norm):
    ts = x_ref.shape[1]

    @pl.when(pl.program_id(1) == 0)
    def _():
        carry_s[...] = jnp.zeros_like(carry_s)

    sub = act_s.shape[0]
    row_ids = lax.broadcasted_iota(jnp.int32, (sub, d_ff), 0)
    for r0 in range(0, ts, sub):
        rows = slice(r0, r0 + sub)
        x = x_ref[0, rows, :]
        h = _rmsnorm(x, norm_ref[layer:layer + 1, :]).astype(BF16)
        gu = _dot(h, wup_ref[...])
        g = gu[:, :d_ff]
        conv = _causal_conv3(g, carry_s[...], cw_ref[:, layer, :], row_ids)
        carry_s[...] = g[sub - SUBLANES:sub, :]
        act_s[...] = (conv * jax.nn.sigmoid(conv) * gu[:, d_ff:]).astype(BF16)
        y = x + _dot(act_s[...], wdn_ref[...])
        if final_norm:
            y = _rmsnorm(y, fnorm_ref[...])
        out_ref[0, rows, :] = y


def _nbytes(shape, dtype):
    n = 1
    for s in shape:
        n *= s
    return n * jnp.dtype(dtype).itemsize


def _layer_param(stacked, layer):
    spec = pl.BlockSpec((None,) + stacked.shape[1:], lambda b, t: (layer, 0, 0),
                        pipeline_mode=pl.Buffered(1))
    return stacked, spec, _nbytes(stacked.shape[1:], stacked.dtype)


def _whole_param(arr):
    spec = pl.BlockSpec(arr.shape, lambda b, t: (0,) * arr.ndim, pipeline_mode=pl.Buffered(1))
    return arr, spec, _nbytes(arr.shape, arr.dtype)


def _conv_param(conv_w):
    return _whole_param(jnp.swapaxes(conv_w, 0, 1))


def _weight_param(w, layer):
    return _layer_param(w, layer) if w.ndim == 3 else _whole_param(w)


def _cast_blocks(rows, steps):
    for n in range(steps, 0, -1):
        if steps % n == 0 and rows % n == 0 and (rows // n) % BF16_SUBLANES == 0:
            return n
    return None


def _with_casts(body, n_in, n_cast):
    def kernel(*refs):
        ins, srcs = refs[:n_in], refs[n_in:n_in + n_cast]
        out, dsts = refs[n_in + n_cast], refs[n_in + n_cast + 1:n_in + 2 * n_cast + 1]
        scratch = refs[n_in + 2 * n_cast + 1:]
        body(*ins, out, *scratch)
        for src, dst in zip(srcs, dsts):
            dst[...] = src[...].astype(dst.dtype)
    return kernel


def _call(body, x, params, scratch, name, temp_bytes, casts=(), ts=SEQ_TILE):
    bsz, seq, d = x.shape
    assert seq % ts == 0 and ts % LANES == 0 and LANES % CHUNK == 0
    n_t = seq // ts
    steps = bsz * n_t
    tile = pl.BlockSpec((1, ts, d), lambda b, t: (b, t, 0))
    arrays = [p[0] for p in params]
    in_specs = [tile] + [p[1] for p in params]
    out_shape = [jax.ShapeDtypeStruct(x.shape, x.dtype)]
    out_specs = [tile]
    streamed = 2 * 2 * _nbytes((ts, d), F32)
    for w, layer in casts:
        rows, cols = w.shape[1:]
        n_blk = _cast_blocks(rows, steps)
        rep = steps // n_blk
        arrays.append(w)
        in_specs.append(pl.BlockSpec((None, rows // n_blk, cols),
                                     lambda b, t, layer=layer, rep=rep: (layer, (b * n_t + t) // rep, 0)))
        out_shape.append(jax.ShapeDtypeStruct((rows, cols), BF16))
        out_specs.append(pl.BlockSpec((rows // n_blk, cols),
                                      lambda b, t, rep=rep: ((b * n_t + t) // rep, 0)))
        streamed += 2 * _nbytes((rows // n_blk, cols), F32) + 2 * _nbytes((rows // n_blk, cols), BF16)
    resident = sum(p[2] for p in params)
    scratch_bytes = sum(_nbytes(s.shape, s.dtype) for s in scratch)
    fixed = resident + scratch_bytes + streamed
    assert fixed < SCOPED_VMEM_CAP_BYTES, fixed
    vmem = min(fixed + temp_bytes, SCOPED_VMEM_CAP_BYTES)
    outs = pl.pallas_call(
        _with_casts(body, 1 + len(params), len(casts)),
        out_shape=out_shape,
        grid=(bsz, n_t),
        in_specs=in_specs,
        out_specs=out_specs,
        scratch_shapes=scratch,
        compiler_params=pltpu.CompilerParams(
            dimension_semantics=("arbitrary", "arbitrary"), vmem_limit_bytes=vmem),
        name=name,
    )(x, *arrays)
    return outs[0], list(outs[1:])


def _castable(w, layer, x, ts):
    steps = x.shape[0] * (x.shape[1] // ts)
    return (w, layer) if _cast_blocks(w.shape[1], steps) else None


def _gla_layer(x, j, norm, w_in, w_gate_up, b_gate, gn, w_out, casts):
    key_dim = w_gate_up.shape[-1]
    value_dim = gn.shape[-1]
    head_k, head_v = key_dim // GLA_HEADS, value_dim // GLA_HEADS
    ts = SEQ_TILE
    w_in_t = jnp.swapaxes(w_in, 1, 2)
    params = [_whole_param(norm), _layer_param(w_in_t, j), _layer_param(w_gate_up, j),
              _whole_param(b_gate), _whole_param(gn), _layer_param(w_out, j)]
    scratch = [
        pltpu.VMEM((ts, x.shape[-1]), BF16),
        pltpu.VMEM((ts, key_dim), BF16),
        pltpu.VMEM((ts, key_dim), F32),
        pltpu.VMEM((ts, value_dim), BF16),
        pltpu.VMEM((ts, key_dim), F32),
        pltpu.VMEM((ts, key_dim), F32),
        pltpu.VMEM((ts, key_dim), F32),
        pltpu.VMEM((key_dim, ts), F32),
        pltpu.VMEM((ts, value_dim), F32),
        pltpu.VMEM((ts, value_dim), F32),
        pltpu.VMEM((ts // CHUNK, GLA_HEADS, head_k, head_k), F32),
        pltpu.VMEM((GLA_HEADS, head_k, head_v), F32),
    ]
    temp = 6 * _nbytes((ts, value_dim), F32)
    kern = functools.partial(_gla_kernel, layer=j, key_dim=key_dim, value_dim=value_dim)
    return _call(kern, x, params, scratch, "gla_layer", temp, casts, ts=GLA_SEQ_TILE)


def _sconv_layer(x, j, norm, w_in, conv_w, w_out, casts):
    conv_dim = conv_w.shape[-1]
    params = [_whole_param(norm), _weight_param(w_in, j), _conv_param(conv_w), _weight_param(w_out, j)]
    scratch = [pltpu.VMEM((SUBLANES, conv_dim), F32)]
    temp = 6 * _nbytes((SCONV_SEQ_TILE, conv_dim), F32)
    kern = functools.partial(_sconv_kernel, layer=j, conv_dim=conv_dim)
    return _call(kern, x, params, scratch, "sconv_layer", temp, casts, ts=SCONV_SEQ_TILE)


def _ffn_layer(x, i, norm, w_up, conv_w, w_down, final_norm_w, final_norm, casts):
    d = x.shape[-1]
    d_ff = conv_w.shape[-1]
    params = [_whole_param(norm), _weight_param(w_up, i), _conv_param(conv_w), _weight_param(w_down, i),
              _whole_param(final_norm_w.reshape(1, d))]
    scratch = [
        pltpu.VMEM((FFN_ROW_BLOCK, d_ff), BF16),
        pltpu.VMEM((SUBLANES, d_ff), F32),
    ]
    temp = 4 * _nbytes((SEQ_TILE, d_ff), F32) + 3 * _nbytes((SEQ_TILE, d), F32)
    kern = functools.partial(_ffn_kernel, layer=i, d_ff=d_ff, final_norm=final_norm)
    return _call(kern, x, params, scratch, "ffn_final" if final_norm else "ffn_layer", temp, casts,
                 ts=FFN_SEQ_TILE)


def kernel(x, a_norm, a_w_in, a_w_gate_up, a_b_gate, a_gn, a_w_out, b_norm, b_w_in, b_conv, b_w_out,
           f_norm, f_w_up, f_conv, f_w_down, final_norm):
    depth = f_norm.shape[0]
    plan = []
    for i in range(depth):
        plan.append(("gla" if i % N_MIXERS == 0 else "sconv", i // N_MIXERS))
        plan.append(("ffn", i))
    tile_of = {"gla": GLA_SEQ_TILE, "ffn": FFN_SEQ_TILE, "sconv": SCONV_SEQ_TILE}
    weights_of = {"gla": (), "sconv": (b_w_in, b_w_out), "ffn": (f_w_up, f_w_down)}

    ready = {}
    for pos, (kind, idx) in enumerate(plan):
        casts = []
        if pos + 1 < len(plan):
            nxt_kind, nxt_idx = plan[pos + 1]
            casts = [_castable(w, nxt_idx, x, tile_of[kind]) for w in weights_of[nxt_kind]]
            casts = casts if all(c is not None for c in casts) else []
        mats = ready.get(pos, weights_of[kind])
        if kind == "gla":
            x, done = _gla_layer(x, idx, a_norm, a_w_in, a_w_gate_up, a_b_gate, a_gn, a_w_out, casts)
        elif kind == "sconv":
            x, done = _sconv_layer(x, idx, b_norm, mats[0], b_conv, mats[1], casts)
        else:
            x, done = _ffn_layer(x, idx, f_norm, mats[0], f_conv, mats[1], final_norm,
                                 final_norm=(pos == len(plan) - 1), casts=casts)
        if done:
            ready[pos + 1] = done
    return x
```

```python
import functools

import jax
import jax.numpy as jnp
from jax import lax
from jax.experimental import pallas as pl
from jax.experimental.pallas import tpu as pltpu

F32 = jnp.float32
BF16 = jnp.bfloat16

CHUNK = 64
GLA_HEADS = 4
GATE_NORMALIZER = 16.0
RMS_EPS = 1e-6
N_MIXERS = 2
CONV_WIDTH = 3

SUBLANES = 8
BF16_SUBLANES = 16
LANES = 128
SCOPED_VMEM_CAP_BYTES = 60000 * 1024

SEQ_TILE = 512
SCONV_SEQ_TILE = 1024
SCONV_ROW_BLOCK = 256
FFN_SEQ_TILE = 1024
FFN_ROW_BLOCK = 256
GLA_SEQ_TILE = 1024


def _dot(a, b):
    return lax.dot_general(a, b, (((1,), (0,)), ((), ())), preferred_element_type=F32)


def _dot_nt(a, bt):
    return lax.dot_general(a, bt, (((1,), (1,)), ((), ())), preferred_element_type=F32)


def _rmsnorm(x, g):
    ms = jnp.mean(x * x, axis=-1, keepdims=True)
    return x * lax.rsqrt(ms + RMS_EPS) * g


def _causal_conv3(z, prev, w, row_ids):
    assert w.shape[0] == CONV_WIDTH
    last, before = prev[SUBLANES - 1:SUBLANES, :], prev[SUBLANES - 2:SUBLANES - 1, :]
    zm1 = jnp.where(row_ids == 0, last, pltpu.roll(z, 1, 0))
    zm2 = jnp.where(row_ids == 0, before, jnp.where(row_ids == 1, last, pltpu.roll(z, 2, 0)))
    return w[0:1, :] * zm2 + w[1:2, :] * zm1 + w[2:3, :] * z


def _log_sigmoid(x):
    return jnp.minimum(x, 0.0) - jnp.log(1.0 + jnp.exp(-jnp.abs(x)))


def _gla_kernel(x_ref, norm_ref, wint_ref, wgu_ref, bg_ref, gn_ref, wo_ref, out_ref, *scratch, **static):
    state_s = scratch[-1]
    sub = scratch[0].shape[0]

    @pl.when(pl.program_id(1) == 0)
    def _():
        state_s[...] = jnp.zeros_like(state_s)

    for r0 in range(0, x_ref.shape[1], sub):
        rows = slice(r0, r0 + sub)
        out_ref[0, rows, :] = _gla_block(x_ref[0, rows, :], norm_ref, wint_ref, wgu_ref, bg_ref, gn_ref,
                                         wo_ref, *scratch, **static)


def _gla_block(x, norm_ref, wint_ref, wgu_ref, bg_ref, gn_ref, wo_ref,
               h_s, q_s, k_s, v_s, la_s, cum_s, kd_s, kdt_s, gate_s, o_s, dcol_s, state_s,
               *, layer, key_dim, value_dim):
    ts = x.shape[0]
    vec = slice(layer, layer + 1)
    n_chunks = ts // CHUNK
    head_k = key_dim // GLA_HEADS
    head_v = value_dim // GLA_HEADS
    main = 2 * key_dim + 2 * value_dim

    h_s[...] = _rmsnorm(x, norm_ref[vec, :]).astype(BF16)
    gl = _dot_nt(h_s[...], wint_ref[main:, :])
    qk = _dot_nt(h_s[...], wint_ref[0:2 * key_dim, :])
    q_s[...] = (qk[:, :key_dim] * (head_k ** -0.5)).astype(BF16)
    k_s[...] = qk[:, key_dim:]
    logits = _dot(gl.astype(BF16), wgu_ref[...]) + bg_ref[vec, :]
    la_s[...] = _log_sigmoid(logits) * (1.0 / GATE_NORMALIZER)
    v_s[...] = _dot_nt(h_s[...], wint_ref[2 * key_dim:2 * key_dim + value_dim, :]).astype(BF16)

    r_i = lax.broadcasted_iota(jnp.int32, (CHUNK, CHUNK), 0)
    c_i = lax.broadcasted_iota(jnp.int32, (CHUNK, CHUNK), 1)
    tri = jnp.where(r_i >= c_i, 1.0, 0.0).astype(BF16)

    for c in range(n_chunks):
        rows = slice(c * CHUNK, (c + 1) * CHUNK)
        la = la_s[rows, :]
        la_hi = la.astype(BF16)
        la_lo = (la - la_hi.astype(F32)).astype(BF16)
        cum_s[rows, :] = _dot(tri, la_hi) + _dot(tri, la_lo)
    r = _dot_nt(h_s[...], wint_ref[2 * key_dim + value_dim:main, :])
    gate_s[...] = gn_ref[vec, :] * (r * jax.nn.sigmoid(r))
    for c in range(n_chunks):
        rows = slice(c * CHUNK, (c + 1) * CHUNK)
        cum = cum_s[rows, :]
        tot = cum[CHUNK - 1:CHUNK, :]
        kd_s[rows, :] = k_s[rows, :] * jnp.exp(tot - cum)
        decay = jnp.exp(tot)
        for hd in range(GLA_HEADS):
            ks = slice(hd * head_k, (hd + 1) * head_k)
            dcol_s[c, hd] = jnp.broadcast_to(decay[:, ks], (head_k, head_k)).T
    kdt_s[...] = kd_s[...].T
    lane = lax.broadcasted_iota(jnp.int32, (head_k, LANES), 1)

    def update(c, hd):
        blk = slice((c * CHUNK) // LANES * LANES, ((c * CHUNK) // LANES + 1) * LANES)
        lo = (c * CHUNK) % LANES
        if lo == 0:
            in_chunk = lane < CHUNK
        elif lo + CHUNK == LANES:
            in_chunk = lane >= lo
        else:
            in_chunk = (lane >= lo) & (lane < lo + CHUNK)
        ks = slice(hd * head_k, (hd + 1) * head_k)
        vs = slice(hd * head_v, (hd + 1) * head_v)
        k_dec_t = jnp.where(in_chunk, kdt_s[ks, blk], 0.0).astype(BF16)
        return _dot(k_dec_t, v_s[blk, vs])

    states = [state_s[hd] for hd in range(GLA_HEADS)]
    nxt = [update(0, hd) for hd in range(GLA_HEADS)]
    for c in range(n_chunks):
        rows = slice(c * CHUNK, (c + 1) * CHUNK)
        cur = nxt
        if c + 1 < n_chunks:
            nxt = [update(c + 1, hd) for hd in range(GLA_HEADS)]
        for hd in range(GLA_HEADS):
            ks = slice(hd * head_k, (hd + 1) * head_k)
            vs = slice(hd * head_v, (hd + 1) * head_v)
            d_col = jnp.concatenate([dcol_s[c, hd]] * (head_v // head_k), axis=1)
            states[hd] = states[hd] * d_col + cur[hd]
            o_s[rows, vs] = _dot(q_s[rows, ks], states[hd].astype(BF16))
    for hd in range(GLA_HEADS):
        state_s[hd] = states[hd]

    parts = []
    for hd in range(GLA_HEADS):
        vs = slice(hd * head_v, (hd + 1) * head_v)
        o = o_s[:, vs]
        ms = jnp.mean(o * o, axis=-1, keepdims=True)
        parts.append((o * lax.rsqrt(ms + RMS_EPS) * gate_s[:, vs]).astype(BF16))
    y = jnp.concatenate(parts, axis=1)
    return x + _dot(y, wo_ref[...])


def _sconv_kernel(x_ref, norm_ref, win_ref, cw_ref, wo_ref, out_ref, carry_s, *, layer, conv_dim):
    ts = x_ref.shape[1]

    @pl.when(pl.program_id(1) == 0)
    def _():
        carry_s[...] = jnp.zeros_like(carry_s)

    sub = SCONV_ROW_BLOCK
    row_ids = lax.broadcasted_iota(jnp.int32, (sub, conv_dim), 0)
    for r0 in range(0, ts, sub):
        rows = slice(r0, r0 + sub)
        x = x_ref[0, rows, :]
        h = _rmsnorm(x, norm_ref[layer:layer + 1, :]).astype(BF16)
        ch = _dot(h, win_ref[:, conv_dim:])
        z = ch[:, :conv_dim] * ch[:, conv_dim:]
        conv = _causal_conv3(z, carry_s[...], cw_ref[:, layer, :], row_ids)
        carry_s[...] = z[sub - SUBLANES:sub, :]
        b_gate = _dot(h, win_ref[:, 0:conv_dim])
        y = (b_gate * conv).astype(BF16)
        out_ref[0, rows, :] = x + _dot(y, wo_ref[...])


def _ffn_kernel(x_ref, norm_ref, wup_ref, cw_ref, wdn_ref, fnorm_ref, out_ref, act_s, carry_s,
                *, layer, d_ff, final_norm):
    ts = x_ref.shape[1]

    @pl.when(pl.program_id(1) == 0)
    def _():
        carry_s[...] = jnp.zeros_like(carry_s)

    sub = act_s.shape[0]
    row_ids = lax.broadcasted_iota(jnp.int32, (sub, d_ff), 0)
    for r0 in range(0, ts, sub):
        rows = slice(r0, r0 + sub)
        x = x_ref[0, rows, :]
        h = _rmsnorm(x, norm_ref[layer:layer + 1, :]).astype(BF16)
        gu = _dot(h, wup_ref[...])
        g = gu[:, :d_ff]
        conv = _causal_conv3(g, carry_s[...], cw_ref[:, layer, :], row_ids)
        carry_s[...] = g[sub - SUBLANES:sub, :]
        act_s[...] = (conv * jax.nn.sigmoid(conv) * gu[:, d_ff:]).astype(BF16)
        y = x + _dot(act_s[...], wdn_ref[...])
        if final_norm:
            y = _rmsnorm(y, fnorm_ref[...])
        out_ref[0, rows, :] = y


def _nbytes(shape, dtype):
    n = 1
    for s in shape:
        n *= s
    return n * jnp.dtype(dtype).itemsize


def _layer_param(stacked, layer):
    spec = pl.BlockSpec((None,) + stacked.shape[1:], lambda b, t: (layer, 0, 0),
                        pipeline_mode=pl.Buffered(1))
    return stacked, spec, _nbytes(stacked.shape[1:], stacked.dtype)


def _whole_param(arr):
    spec = pl.BlockSpec(arr.shape, lambda b, t: (0,) * arr.ndim, pipeline_mode=pl.Buffered(1))
    return arr, spec, _nbytes(arr.shape, arr.dtype)


def _conv_param(conv_w):
    return _whole_param(jnp.swapaxes(conv_w, 0, 1))


def _weight_param(w, layer):
    return _layer_param(w, layer) if w.ndim == 3 else _whole_param(w)


def _cast_blocks(rows, steps):
    for n in range(steps, 0, -1):
        if steps % n == 0 and rows % n == 0 and (rows // n) % BF16_SUBLANES == 0:
            return n
    return None


def _with_casts(body, n_in, n_cast):
    def kernel(*refs):
        ins, srcs = refs[:n_in], refs[n_in:n_in + n_cast]
        out, dsts = refs[n_in + n_cast], refs[n_in + n_cast + 1:n_in + 2 * n_cast + 1]
        scratch = refs[n_in + 2 * n_cast + 1:]
        body(*ins, out, *scratch)
        for src, dst in zip(srcs, dsts):
            dst[...] = src[...].astype(dst.dtype)
    return kernel


def _call(body, x, params, scratch, name, temp_bytes, casts=(), ts=SEQ_TILE):
    bsz, seq, d = x.shape
    assert seq % ts == 0 and ts % LANES == 0 and LANES % CHUNK == 0
    n_t = seq // ts
    steps = bsz * n_t
    tile = pl.BlockSpec((1, ts, d), lambda b, t: (b, t, 0))
    arrays = [p[0] for p in params]
    in_specs = [tile] + [p[1] for p in params]
    out_shape = [jax.ShapeDtypeStruct(x.shape, x.dtype)]
    out_specs = [tile]
    streamed = 2 * 2 * _nbytes((ts, d), F32)
    for w, layer in casts:
        rows, cols = w.shape[1:]
        n_blk = _cast_blocks(rows, steps)
        rep = steps // n_blk
        arrays.append(w)
        in_specs.append(pl.BlockSpec((None, rows // n_blk, cols),
                                     lambda b, t, layer=layer, rep=rep: (layer, (b * n_t + t) // rep, 0)))
        out_shape.append(jax.ShapeDtypeStruct((rows, cols), BF16))
        out_specs.append(pl.BlockSpec((rows // n_blk, cols),
                                      lambda b, t, rep=rep: ((b * n_t + t) // rep, 0)))
        streamed += 2 * _nbytes((rows // n_blk, cols), F32) + 2 * _nbytes((rows // n_blk, cols), BF16)
    resident = sum(p[2] for p in params)
    scratch_bytes = sum(_nbytes(s.shape, s.dtype) for s in scratch)
    fixed = resident + scratch_bytes + streamed
    assert fixed < SCOPED_VMEM_CAP_BYTES, fixed
    vmem = min(fixed + temp_bytes, SCOPED_VMEM_CAP_BYTES)
    outs = pl.pallas_call(
        _with_casts(body, 1 + len(params), len(casts)),
        out_shape=out_shape,
        grid=(bsz, n_t),
        in_specs=in_specs,
        out_specs=out_specs,
        scratch_shapes=scratch,
        compiler_params=pltpu.CompilerParams(
            dimension_semantics=("arbitrary", "arbitrary"), vmem_limit_bytes=vmem),
        name=name,
    )(x, *arrays)
    return outs[0], list(outs[1:])


def _castable(w, layer, x, ts):
    steps = x.shape[0] * (x.shape[1] // ts)
    return (w, layer) if _cast_blocks(w.shape[1], steps) else None


def _gla_layer(x, j, norm, w_in, w_gate_up, b_gate, gn, w_out, casts):
    key_dim = w_gate_up.shape[-1]
    value_dim = gn.shape[-1]
    head_k, head_v = key_dim // GLA_HEADS, value_dim // GLA_HEADS
    ts = SEQ_TILE
    w_in_t = jnp.swapaxes(w_in, 1, 2)
    params = [_whole_param(norm), _layer_param(w_in_t, j), _layer_param(w_gate_up, j),
              _whole_param(b_gate), _whole_param(gn), _layer_param(w_out, j)]
    scratch = [
        pltpu.VMEM((ts, x.shape[-1]), BF16),
        pltpu.VMEM((ts, key_dim), BF16),
        pltpu.VMEM((ts, key_dim), F32),
        pltpu.VMEM((ts, value_dim), BF16),
        pltpu.VMEM((ts, key_dim), F32),
        pltpu.VMEM((ts, key_dim), F32),
        pltpu.VMEM((ts, key_dim), F32),
        pltpu.VMEM((key_dim, ts), F32),
        pltpu.VMEM((ts, value_dim), F32),
        pltpu.VMEM((ts, value_dim), F32),
        pltpu.VMEM((ts // CHUNK, GLA_HEADS, head_k, head_k), F32),
        pltpu.VMEM((GLA_HEADS, head_k, head_v), F32),
    ]
    temp = 6 * _nbytes((ts, value_dim), F32)
    kern = functools.partial(_gla_kernel, layer=j, key_dim=key_dim, value_dim=value_dim)
    return _call(kern, x, params, scratch, "gla_layer", temp, casts, ts=GLA_SEQ_TILE)


def _sconv_layer(x, j, norm, w_in, conv_w, w_out, casts):
    conv_dim = conv_w.shape[-1]
    params = [_whole_param(norm), _weight_param(w_in, j), _conv_param(conv_w), _weight_param(w_out, j)]
    scratch = [pltpu.VMEM((SUBLANES, conv_dim), F32)]
    temp = 6 * _nbytes((SCONV_SEQ_TILE, conv_dim), F32)
    kern = functools.partial(_sconv_kernel, layer=j, conv_dim=conv_dim)
    return _call(kern, x, params, scratch, "sconv_layer", temp, casts, ts=SCONV_SEQ_TILE)


def _ffn_layer(x, i, norm, w_up, conv_w, w_down, final_norm_w, final_norm, casts):
    d = x.shape[-1]
    d_ff = conv_w.shape[-1]
    params = [_whole_param(norm), _weight_param(w_up, i), _conv_param(conv_w), _weight_param(w_down, i),
              _whole_param(final_norm_w.reshape(1, d))]
    scratch = [
        pltpu.VMEM((FFN_ROW_BLOCK, d_ff), BF16),
        pltpu.VMEM((SUBLANES, d_ff), F32),
    ]
    temp = 4 * _nbytes((SEQ_TILE, d_ff), F32) + 3 * _nbytes((SEQ_TILE, d), F32)
    kern = functools.partial(_ffn_kernel, layer=i, d_ff=d_ff, final_norm=final_norm)
    return _call(kern, x, params, scratch, "ffn_final" if final_norm else "ffn_layer", temp, casts,
                 ts=FFN_SEQ_TILE)


def kernel(x, a_norm, a_w_in, a_w_gate_up, a_b_gate, a_gn, a_w_out, b_norm, b_w_in, b_conv, b_w_out,
           f_norm, f_w_up, f_conv, f_w_down, final_norm):
    depth = f_norm.shape[0]
    plan = []
    for i in range(depth):
        plan.append(("gla" if i % N_MIXERS == 0 else "sconv", i // N_MIXERS))
        plan.append(("ffn", i))
    tile_of = {"gla": GLA_SEQ_TILE, "ffn": FFN_SEQ_TILE, "sconv": SCONV_SEQ_TILE}
    weights_of = {"gla": (), "sconv": (b_w_in, b_w_out), "ffn": (f_w_up, f_w_down)}

    ready = {}
    for pos, (kind, idx) in enumerate(plan):
        casts = []
        if pos + 1 < len(plan):
            nxt_kind, nxt_idx = plan[pos + 1]
            casts = [_castable(w, nxt_idx, x, tile_of[kind]) for w in weights_of[nxt_kind]]
            casts = casts if all(c is not None for c in casts) else []
        mats = ready.get(pos, weights_of[kind])
        if kind == "gla":
            x, done = _gla_layer(x, idx, a_norm, a_w_in, a_w_gate_up, a_b_gate, a_gn, a_w_out, casts)
        elif kind == "sconv":
            x, done = _sconv_layer(x, idx, b_norm, mats[0], b_conv, mats[1], casts)
        else:
            x, done = _ffn_layer(x, idx, f_norm, mats[0], f_conv, mats[1], final_norm,
                                 final_norm=(pos == len(plan) - 1), casts=casts)
        if done:
            ready[pos + 1] = done
    return x
```
